```python
import math
import jax, jax.numpy as jnp
from jax import lax
import numpy as np

D_MODEL = 1024
BATCH = 16
SEQ = 4096
DEPTH = 1

MEM_LEN = 256
POOL_WIDTH = D_MODEL // 2
POOL_WINDOWS = (2, 4, 8, 16)
POOL_GROUP = POOL_WIDTH // len(POOL_WINDOWS)
ATTN_WIDTH = D_MODEL - POOL_WIDTH
ATTN_HEAD_DIM = 64
ATTN_HEADS = ATTN_WIDTH // ATTN_HEAD_DIM
IN_WIDTH = POOL_WIDTH + 3 * ATTN_WIDTH
MOBA_BLOCK = 256
MOBA_TOPK = 3
Q_CHUNK = 16
XATTN_HEADS = 4
XATTN_HEAD_DIM = D_MODEL // XATTN_HEADS
D_FF = -(-8 * D_MODEL // (3 * 256)) * 256
DEEPNORM_ALPHA = (2.0 * DEPTH) ** 0.25
DEEPNORM_BETA = (8.0 * DEPTH) ** -0.25
LN_EPS = 1e-5

kernel_name = "hymba_pool_moba_deepnorm_layer"


def layer_norm(x, g, b):
    xf = x.astype(jnp.float32)
    mu = jnp.mean(xf, axis=-1, keepdims=True)
    var = jnp.mean(jnp.square(xf - mu), axis=-1, keepdims=True)
    return ((xf - mu) * lax.rsqrt(var + LN_EPS) * g + b).astype(x.dtype)


def multiscale_pool(u, w_pool, pool_scale):
    B_, S_, _ = u.shape
    uf = u.astype(jnp.float32)
    cs = jnp.pad(jnp.cumsum(uf, axis=1), ((0, 0), (1, 0), (0, 0)))
    t = jnp.arange(S_)
    groups = []
    for g, w in enumerate(POOL_WINDOWS):
        sl = slice(g * POOL_GROUP, (g + 1) * POOL_GROUP)
        c_hi = cs[:, 1:, sl]
        c_lo = jnp.pad(cs[:, :S_ + 1 - w, sl], ((0, 0), (w - 1, 0), (0, 0)))
        count = jnp.minimum(t + 1, w).astype(jnp.float32)[None, :, None]
        groups.append((c_hi - c_lo) / count - uf[..., sl])
    pooled = jnp.stack(groups, axis=2)
    mixed = jnp.einsum('bsgc,gcd->bsgd', pooled, w_pool.astype(jnp.float32))
    return (mixed.reshape(B_, S_, POOL_WIDTH) * pool_scale).astype(u.dtype)


def moba_attention(q, k, v):
    B_, S_, H, Dh = q.shape
    nb = -(-S_ // MOBA_BLOCK)
    s_pad = nb * MOBA_BLOCK
    pad = ((0, 0), (0, s_pad - S_), (0, 0), (0, 0))
    q, k, v = [jnp.pad(a, pad).transpose(0, 2, 1, 3) for a in (q, k, v)]
    kb = k.reshape(B_, H, nb, MOBA_BLOCK, Dh)
    vb = v.reshape(B_, H, nb, MOBA_BLOCK, Dh)
    k_mean = jnp.mean(kb.astype(jnp.float32), axis=3)
    topk = min(MOBA_TOPK, nb - 1)
    scale = ATTN_HEAD_DIM ** -0.5
    n_chunks = -(-S_ // Q_CHUNK)
    gather_blocks = jax.vmap(jax.vmap(lambda tab, ix: tab[ix]))

    def chunk_fn(ci):
        q0 = ci * Q_CHUNK
        blk = q0 // MOBA_BLOCK
        qc = lax.dynamic_slice_in_dim(q, q0, Q_CHUNK, axis=2)
        k_own = lax.dynamic_slice_in_dim(k, blk * MOBA_BLOCK, MOBA_BLOCK, axis=2)
        v_own = lax.dynamic_slice_in_dim(v, blk * MOBA_BLOCK, MOBA_BLOCK, axis=2)
        qpos = q0 + jnp.arange(Q_CHUNK)
        kpos = blk * MOBA_BLOCK + jnp.arange(MOBA_BLOCK)
        s_own = jnp.einsum('bhqd,bhkd->bhqk', qc, k_own,
                           preferred_element_type=jnp.float32) * scale
        s_own = jnp.where(kpos[None, :] <= qpos[:, None], s_own, -jnp.inf)
        if topk > 0:
            gate = jnp.einsum('bhqd,bhnd->bhqn', qc.astype(jnp.float32), k_mean)
            gate = jnp.where(jnp.arange(nb) < blk, gate, -jnp.inf)
            _, idx = lax.top_k(gate, topk)
            valid = idx < blk
            k_sel = gather_blocks(kb, idx)
            v_sel = gather_blocks(vb, idx)
            s_sel = jnp.einsum('bhqd,bhqnkd->bhqnk', qc, k_sel,
                               preferred_element_type=jnp.float32) * scale
            s_sel = jnp.where(valid[..., None], s_sel, -jnp.inf)
            s_sel = s_sel.reshape(B_, H, Q_CHUNK, topk * MOBA_BLOCK)
            p = jax.nn.softmax(jnp.concatenate([s_sel, s_own], axis=-1), axis=-1)
            p_sel = p[..., :topk * MOBA_BLOCK].reshape(B_, H, Q_CHUNK, topk, MOBA_BLOCK)
            p_own = p[..., topk * MOBA_BLOCK:]
            o = (jnp.einsum('bhqnk,bhqnkd->bhqd', p_sel.astype(v.dtype), v_sel,
                            preferred_element_type=jnp.float32)
                 + jnp.einsum('bhqk,bhkd->bhqd', p_own.astype(v.dtype), v_own,
                              preferred_element_type=jnp.float32))
        else:
            p_own = jax.nn.softmax(s_own, axis=-1)
            o = jnp.einsum('bhqk,bhkd->bhqd', p_own.astype(v.dtype), v_own,
                           preferred_element_type=jnp.float32)
        return o.astype(q.dtype)

    out = lax.map(chunk_fn, jnp.arange(n_chunks))
    out = out.transpose(1, 0, 3, 2, 4).reshape(B_, n_chunks * Q_CHUNK, H, Dh)
    return out[:, :S_]


def memory_cross_attention(h, mem, w_xq, w_xkv, w_xo):
    B_, S_, D = h.shape
    M = mem.shape[1]
    q = (h @ w_xq).reshape(B_, S_, XATTN_HEADS, XATTN_HEAD_DIM)
    kv = mem @ w_xkv
    k = kv[..., :D].reshape(B_, M, XATTN_HEADS, XATTN_HEAD_DIM)
    v = kv[..., D:].reshape(B_, M, XATTN_HEADS, XATTN_HEAD_DIM)
    s = jnp.einsum('bshd,bmhd->bhsm', q, k,
                   preferred_element_type=jnp.float32) * (XATTN_HEAD_DIM ** -0.5)
    p = jax.nn.softmax(s, axis=-1)
    o = jnp.einsum('bhsm,bmhd->bshd', p.astype(v.dtype), v).reshape(B_, S_, D)
    return o @ w_xo


def swiglu(h, w_gate, w_up, w_down):
    return (jax.nn.silu(h @ w_gate) * (h @ w_up)) @ w_down


def setup_inputs(seed: int = 0) -> dict:
    key = jax.random.key(seed)
    ks = jax.random.split(key, 20)
    f32 = jnp.float32
    nrm = lambda k, shape, s: jax.random.normal(k, shape, f32) * s
    L, D = DEPTH, D_MODEL
    return {
        "x": nrm(ks[0], (BATCH, SEQ, D), 1.0),
        "mem": nrm(ks[1], (BATCH, MEM_LEN, D), 1.0),
        "w_in": nrm(ks[2], (L, D, IN_WIDTH), D ** -0.5),
        "w_pool": nrm(ks[3], (L, len(POOL_WINDOWS), POOL_GROUP, POOL_GROUP), POOL_GROUP ** -0.5),
        "pool_scale": 1.0 + nrm(ks[4], (L, POOL_WIDTH), 0.02),
        "w_out": nrm(ks[5], (L, D, D), D ** -0.5 * DEEPNORM_BETA),
        "ln1_g": 1.0 + nrm(ks[6], (L, D), 0.02),
        "ln1_b": nrm(ks[7], (L, D), 0.02),
        "w_xq": nrm(ks[8], (L, D, D), D ** -0.5),
        "w_xkv": nrm(ks[9], (L, D, 2 * D), D ** -0.5),
        "w_xo": nrm(ks[10], (L, D, D), D ** -0.5 * DEEPNORM_BETA),
        "ln2_g": 1.0 + nrm(ks[11], (L, D), 0.02),
        "ln2_b": nrm(ks[12], (L, D), 0.02),
        "w_gate": nrm(ks[13], (L, D, D_FF), D ** -0.5),
        "w_up": nrm(ks[14], (L, D, D_FF), D ** -0.5),
        "w_down": nrm(ks[15], (L, D_FF, D), D_FF ** -0.5 * DEEPNORM_BETA),
        "ln3_g": 1.0 + nrm(ks[16], (L, D), 0.02),
        "ln3_b": nrm(ks[17], (L, D), 0.02),
    }


def reference(x, mem, w_in, w_pool, pool_scale, w_out, ln1_g, ln1_b, w_xq, w_xkv, w_xo,
              ln2_g, ln2_b, w_gate, w_up, w_down, ln3_g, ln3_b):
    B_, S_, _ = x.shape
    h = x
    for l in range(DEPTH):
        z = h @ w_in[l]
        u = z[..., :POOL_WIDTH]
        q = z[..., POOL_WIDTH:POOL_WIDTH + ATTN_WIDTH].reshape(B_, S_, ATTN_HEADS, ATTN_HEAD_DIM)
        k = z[..., POOL_WIDTH + ATTN_WIDTH:POOL_WIDTH + 2 * ATTN_WIDTH].reshape(B_, S_, ATTN_HEADS, ATTN_HEAD_DIM)
        v = z[..., POOL_WIDTH + 2 * ATTN_WIDTH:].reshape(B_, S_, ATTN_HEADS, ATTN_HEAD_DIM)
        pool_out = multiscale_pool(u, w_pool[l], pool_scale[l])
        attn_out = moba_attention(q, k, v).reshape(B_, S_, ATTN_WIDTH)
        mix = jnp.concatenate([pool_out, attn_out], axis=-1) @ w_out[l]
        h = layer_norm(DEEPNORM_ALPHA * h + mix, ln1_g[l], ln1_b[l])
        h = layer_norm(DEEPNORM_ALPHA * h + memory_cross_attention(h, mem, w_xq[l], w_xkv[l], w_xo[l]),
                       ln2_g[l], ln2_b[l])
        h = layer_norm(DEEPNORM_ALPHA * h + swiglu(h, w_gate[l], w_up[l], w_down[l]),
                       ln3_g[l], ln3_b[l])
    return h
```

```python
import functools
import math

import jax
import jax.numpy as jnp
from jax import lax
from jax.experimental import pallas as pl
from jax.experimental.pallas import tpu as pltpu

POOL_WINDOWS = (2, 4, 8, 16)
POOL_GROUP = 128
POOL_WIDTH = POOL_GROUP * len(POOL_WINDOWS)
POOL_HALO = max(POOL_WINDOWS)
ATTN_HEAD_DIM = 64
MOBA_BLOCK = 256
MOBA_TOPK = 3
XATTN_HEADS = 4
DEPTH = 1
DEEPNORM_ALPHA = (2.0 * DEPTH) ** 0.25
LN_EPS = 1e-5
LOG2E = math.log2(math.e)

V7X_LANES = 128
V7X_VMEM_BYTES = 64 * 1024 * 1024
HEADS_PER_SLAB = V7X_LANES // ATTN_HEAD_DIM

F32 = jnp.float32
BF16 = jnp.bfloat16
NEG_INF = float("-inf")


def _vmem_limit(resident_bytes):
    return int(min(2 * resident_bytes, V7X_VMEM_BYTES * 7 // 8))


def _const_spec(shape):
    return pl.BlockSpec(shape, lambda *_: (0,) * len(shape), pipeline_mode=pl.Buffered(1))


def _dot(a, b):
    return jnp.dot(a, b, preferred_element_type=F32)


def _layer_norm(y, g, b):
    mu = jnp.mean(y, axis=-1, keepdims=True)
    d = y - mu
    var = jnp.mean(d * d, axis=-1, keepdims=True)
    return d * lax.rsqrt(var + LN_EPS) * g + b


def _proj_pool_kernel(x_ref, w_in_ref, w_pool_ref, scale_ref,
                      pool_ref, q_ref, k_ref, v_ref, ubuf_ref, *, q_scale):
    s_idx = pl.program_id(1)
    ts = x_ref.shape[1]
    aw = q_ref.shape[2]
    xb = x_ref[0].astype(BF16)

    u = _dot(xb, w_in_ref[:, 0:POOL_WIDTH])
    q_ref[0] = (_dot(xb, w_in_ref[:, POOL_WIDTH:POOL_WIDTH + aw]) * q_scale).astype(BF16)
    k_ref[0] = _dot(xb, w_in_ref[:, POOL_WIDTH + aw:POOL_WIDTH + 2 * aw]).astype(BF16)
    v_ref[0] = _dot(xb, w_in_ref[:, POOL_WIDTH + 2 * aw:POOL_WIDTH + 3 * aw]).astype(BF16)

    @pl.when(s_idx == 0)
    def _():
        ubuf_ref[0:POOL_HALO, :] = jnp.zeros((POOL_HALO, POOL_WIDTH), F32)

    ubuf_ref[POOL_HALO:POOL_HALO + ts, :] = u
    pos = s_idx * ts + lax.broadcasted_iota(jnp.int32, (ts, POOL_GROUP), 0)
    for g, w in enumerate(POOL_WINDOWS):
        cols = slice(g * POOL_GROUP, (g + 1) * POOL_GROUP)
        win = ubuf_ref[:, cols]
        span = 1
        while span < w:
            win = win + pltpu.roll(win, span, axis=0)
            span *= 2
        inv_count = 1.0 / jnp.minimum(pos + 1, w).astype(F32)
        pooled = win[POOL_HALO:, :] * inv_count - u[:, cols]
        mixed = _dot(pooled.astype(BF16), w_pool_ref[g]) * scale_ref[:, cols]
        pool_ref[0, :, cols] = mixed.astype(BF16)
    ubuf_ref[0:POOL_HALO, :] = u[ts - POOL_HALO:, :]


def _proj_pool(x, w_in, w_pool, pool_scale, *, ts):
    B, S, D = x.shape
    aw = (w_in.shape[1] - POOL_WIDTH) // 3
    q_scale = ATTN_HEAD_DIM ** -0.5 * LOG2E
    slab = lambda width: pl.BlockSpec((1, ts, width), lambda b, s: (b, s, 0))
    resident = (2 * ts * D * 4 + D * w_in.shape[1] * 2 + 2 * ts * (POOL_WIDTH + 3 * aw) * 2
                + (ts + POOL_HALO) * POOL_WIDTH * 4 + 4 * ts * POOL_WIDTH * 4)
    return pl.pallas_call(
        functools.partial(_proj_pool_kernel, q_scale=q_scale),
        grid=(B, S // ts),
        in_specs=[slab(D), _const_spec(w_in.shape), _const_spec(w_pool.shape),
                  _const_spec(pool_scale.shape)],
        out_specs=[slab(POOL_WIDTH), slab(aw), slab(aw), slab(aw)],
        out_shape=[jax.ShapeDtypeStruct((B, S, POOL_WIDTH), BF16)]
        + [jax.ShapeDtypeStruct((B, S, aw), BF16)] * 3,
        scratch_shapes=[pltpu.VMEM((ts + POOL_HALO, POOL_WIDTH), F32)],
        compiler_params=pltpu.CompilerParams(
            dimension_semantics=("parallel", "arbitrary"),
            vmem_limit_bytes=_vmem_limit(resident)),
    )(x, w_in, w_pool, pool_scale)


def _moba_kernel(q_ref, k_ref, v_ref, o_ref, qt_ref, vt_ref, kmean_ref, bias_ref):
    S = q_ref.shape[1]
    nb = S // MOBA_BLOCK
    blk = MOBA_BLOCK
    hd = ATTN_HEAD_DIM

    for c in range(nb):
        rows = slice(c * blk, (c + 1) * blk)
        qt_ref[:, rows] = q_ref[0, rows, :].astype(F32).T.astype(BF16)
        vt_ref[:, rows] = v_ref[0, rows, :].astype(F32).T.astype(BF16)
        kmean_ref[c:c + 1, :] = jnp.mean(k_ref[0, rows, :].astype(F32), axis=0, keepdims=True)

    chan_head = lax.broadcasted_iota(jnp.int32, (V7X_LANES, blk), 0) // hd
    blk_idx = lax.broadcasted_iota(jnp.int32, (nb, blk), 0)
    causal = (lax.broadcasted_iota(jnp.int32, (blk, blk), 0)
              <= lax.broadcasted_iota(jnp.int32, (blk, blk), 1))

    def q_block(i, carry):
        q0 = pl.multiple_of(i * blk, blk)
        qtf = qt_ref[:, pl.ds(q0, blk)].astype(F32)
        k_own = k_ref[0, pl.ds(q0, blk), :]

        qth, state = [], []
        for h in range(HEADS_PER_SLAB):
            qhf = jnp.where(chan_head == h, qtf, 0.0)
            qth.append(qhf.astype(BF16))

            gate = jnp.dot(kmean_ref[...], qhf, preferred_element_type=F32,
                           precision=lax.Precision.HIGHEST)
            gate = jnp.where(blk_idx < i, gate, NEG_INF)
            rank = jnp.zeros((nb, blk), jnp.int32)
            for jp in range(nb):
                other = gate[jp:jp + 1, :]
                ahead = (other > gate) | ((other == gate) & (jp < blk_idx))
                rank = rank + ahead.astype(jnp.int32)
            chosen = (rank < MOBA_TOPK) & (blk_idx < i)
            bias_ref[h] = jnp.where(chosen, 0.0, NEG_INF)

            s = jnp.where(causal, _dot(k_own, qth[h]), NEG_INF)
            m = jnp.max(s, axis=0, keepdims=True)
            p = jnp.exp2(s - m)
            l = jnp.sum(p, axis=0, keepdims=True)
            acc = _dot(vt_ref[h * hd:(h + 1) * hd, pl.ds(q0, blk)], p.astype(BF16))
            state += [m, l, acc]

        def kv_block(j, st):
            k0 = pl.multiple_of(j * blk, blk)
            kj = k_ref[0, pl.ds(k0, blk), :]
            out = []
            for h in range(HEADS_PER_SLAB):
                m, l, acc = st[3 * h:3 * h + 3]
                s = _dot(kj, qth[h]) + bias_ref[h, pl.ds(j, 1), :]
                m_new = jnp.maximum(m, jnp.max(s, axis=0, keepdims=True))
                alpha = jnp.exp2(m - m_new)
                p = jnp.exp2(s - m_new)
                l = alpha * l + jnp.sum(p, axis=0, keepdims=True)
                pv = _dot(vt_ref[h * hd:(h + 1) * hd, pl.ds(k0, blk)], p.astype(BF16))
                out += [m_new, l, alpha * acc + pv]
            return tuple(out)

        st = lax.fori_loop(0, i, kv_block, tuple(state))
        o_t = jnp.concatenate(
            [st[3 * h + 2] * (1.0 / st[3 * h + 1]) for h in range(HEADS_PER_SLAB)], axis=0)
        o_ref[0, pl.ds(q0, blk), :] = o_t.T.astype(BF16)
        return carry

    lax.fori_loop(0, nb, q_block, 0)


def _moba(q, k, v):
    B, S, aw = q.shape
    nb = S // MOBA_BLOCK
    slab = pl.BlockSpec((1, S, V7X_LANES), lambda b, g: (b, 0, g))
    resident = 8 * S * V7X_LANES * 2 + 2 * S * V7X_LANES * 2 + 16 * MOBA_BLOCK * MOBA_BLOCK * 4
    return pl.pallas_call(
        _moba_kernel,
        grid=(B, aw // V7X_LANES),
        in_specs=[slab, slab, slab],
        out_specs=slab,
        out_shape=jax.ShapeDtypeStruct((B, S, aw), BF16),
        scratch_shapes=[pltpu.VMEM((V7X_LANES, S), BF16), pltpu.VMEM((V7X_LANES, S), BF16),
                        pltpu.VMEM((nb, V7X_LANES), F32),
                        pltpu.VMEM((HEADS_PER_SLAB, nb, MOBA_BLOCK), F32)],
        compiler_params=pltpu.CompilerParams(
            dimension_semantics=("parallel", "parallel"),
            vmem_limit_bytes=_vmem_limit(resident)),
    )(q, k, v)


def _mem_kv_kernel(mem_ref, w_ref, k_ref, v_ref):
    D = mem_ref.shape[2]
    mb = mem_ref[0].astype(BF16)
    k_ref[0] = _dot(mb, w_ref[:, 0:D]).astype(BF16)
    v_ref[0] = _dot(mb, w_ref[:, D:2 * D]).astype(BF16)


def _mem_kv(mem, w_xkv):
    B, M, D = mem.shape
    blk = pl.BlockSpec((1, M, D), lambda b: (b, 0, 0))
    resident = 2 * M * D * 4 + D * 2 * D * 2 + 4 * M * D * 2 + 2 * M * D * 4
    return pl.pallas_call(
        _mem_kv_kernel,
        grid=(B,),
        in_specs=[blk, _const_spec(w_xkv.shape)],
        out_specs=[blk, blk],
        out_shape=[jax.ShapeDtypeStruct((B, M, D), BF16)] * 2,
        compiler_params=pltpu.CompilerParams(
            dimension_semantics=("parallel",), vmem_limit_bytes=_vmem_limit(resident)),
    )(mem, w_xkv)


def _mix_xattn_kernel(x_ref, pool_ref, attn_ref, w_out_ref, g1_ref, b1_ref,
                      w_xq_ref, kx_ref, vx_ref, w_xo_ref, g2_ref, b2_ref, h_ref, *, q_scale):
    D = x_ref.shape[2]
    pw = pool_ref.shape[2]
    xhd = D // XATTN_HEADS
    mix = _dot(pool_ref[0], w_out_ref[0:pw, :]) + _dot(attn_ref[0], w_out_ref[pw:D, :])
    h1 = _layer_norm(DEEPNORM_ALPHA * x_ref[0] + mix, g1_ref[...], b1_ref[...])

    qx = (_dot(h1.astype(BF16), w_xq_ref[...]) * q_scale).astype(BF16)
    heads = []
    for h in range(XATTN_HEADS):
        cols = slice(h * xhd, (h + 1) * xhd)
        s = lax.dot_general(qx[:, cols], kx_ref[0, :, cols], (((1,), (1,)), ((), ())),
                            preferred_element_type=F32)
        p = jnp.exp2(s - jnp.max(s, axis=-1, keepdims=True))
        inv_l = 1.0 / jnp.sum(p, axis=-1, keepdims=True)
        heads.append((_dot(p.astype(BF16), vx_ref[0, :, cols]) * inv_l).astype(BF16))
    xo = _dot(jnp.concatenate(heads, axis=-1), w_xo_ref[...])
    h_ref[0] = _layer_norm(DEEPNORM_ALPHA * h1 + xo, g2_ref[...], b2_ref[...])


def _mix_xattn(x, pool, attn, w_out, g1, b1, w_xq, kx, vx, w_xo, g2, b2, *, ts):
    B, S, D = x.shape
    M = kx.shape[1]
    q_scale = (D // XATTN_HEADS) ** -0.5 * LOG2E
    tile = lambda width: pl.BlockSpec((1, ts, width), lambda b, s: (b, s, 0))
    mem_blk = pl.BlockSpec((1, M, D), lambda b, s: (b, 0, 0))
    vec = _const_spec((1, D))
    resident = (4 * ts * D * 4 + 4 * ts * pool.shape[2] * 2 + 3 * D * D * 2 + 4 * M * D * 2
                + 6 * ts * D * 4)
    return pl.pallas_call(
        functools.partial(_mix_xattn_kernel, q_scale=q_scale),
        grid=(B, S // ts),
        in_specs=[tile(D), tile(pool.shape[2]), tile(attn.shape[2]), _const_spec(w_out.shape),
                  vec, vec, _const_spec(w_xq.shape), mem_blk, mem_blk, _const_spec(w_xo.shape),
                  vec, vec],
        out_specs=tile(D),
        out_shape=jax.ShapeDtypeStruct((B, S, D), F32),
        compiler_params=pltpu.CompilerParams(
            dimension_semantics=("parallel", "parallel"),
            vmem_limit_bytes=_vmem_limit(resident)),
    )(x, pool, attn, w_out, g1, b1, w_xq, kx, vx, w_xo, g2, b2)


def _ffn_kernel(h_ref, w_gate_ref, w_up_ref, w_down_ref, g_ref, b_ref, o_ref, *, ff_chunk):
    h = h_ref[...]
    hb = h.astype(BF16)
    d_ff = w_gate_ref.shape[1]
    y = jnp.zeros(h.shape, F32)
    for c in range(d_ff // ff_chunk):
        cols = slice(c * ff_chunk, (c + 1) * ff_chunk)
        gate = _dot(hb, w_gate_ref[:, cols])
        up = _dot(hb, w_up_ref[:, cols])
        act = gate * (1.0 / (1.0 + jnp.exp(-gate))) * up
        y = y + _dot(act.astype(BF16), w_down_ref[cols, :])
    o_ref[...] = _layer_norm(DEEPNORM_ALPHA * h + y, g_ref[...], b_ref[...])


def _ffn(h, w_gate, w_up, w_down, g, b, *, ts, ff_chunk):
    N, D = h.shape
    d_ff = w_gate.shape[1]
    tile = pl.BlockSpec((ts, D), lambda t: (t, 0))
    vec = _const_spec((1, D))
    resident = 4 * ts * D * 4 + 3 * D * d_ff * 2 + 3 * ts * ff_chunk * 4 + 2 * ts * D * 4
    return pl.pallas_call(
        functools.partial(_ffn_kernel, ff_chunk=ff_chunk),
        grid=(N // ts,),
        in_specs=[tile, _const_spec(w_gate.shape), _const_spec(w_up.shape),
                  _const_spec(w_down.shape), vec, vec],
        out_specs=tile,
        out_shape=jax.ShapeDtypeStruct((N, D), F32),
        compiler_params=pltpu.CompilerParams(
            dimension_semantics=("parallel",), vmem_limit_bytes=_vmem_limit(resident)),
    )(h, w_gate, w_up, w_down, g, b)


def kernel(x, mem, w_in, w_pool, pool_scale, w_out, ln1_g, ln1_b, w_xq, w_xkv, w_xo,
           ln2_g, ln2_b, w_gate, w_up, w_down, ln3_g, ln3_b):
    B, S, D = x.shape
    assert w_in.shape[0] == DEPTH and S % MOBA_BLOCK == 0
    assert (w_in.shape[2] - POOL_WIDTH) % (3 * V7X_LANES) == 0
    h = x
    for l in range(DEPTH):
        wb = lambda w: w[l].astype(BF16)
        row = lambda p: p[l].reshape(1, -1)
        pool, q, k, v = _proj_pool(h, wb(w_in), wb(w_pool), row(pool_scale), ts=1024)
        attn = _moba(q, k, v)
        kx, vx = _mem_kv(mem, wb(w_xkv))
        h = _mix_xattn(h, pool, attn, wb(w_out), row(ln1_g), row(ln1_b), wb(w_xq), kx, vx,
                       wb(w_xo), row(ln2_g), row(ln2_b), ts=512)
        h = _ffn(h.reshape(B * S, D), wb(w_gate), wb(w_up), wb(w_down),
                 row(ln3_g), row(ln3_b), ts=512, ff_chunk=256).reshape(B, S, D)
    return h
```

```python
import functools
import math

import jax
import jax.numpy as jnp
from jax import lax
from jax.experimental import pallas as pl
from jax.experimental.pallas import tpu as pltpu

POOL_WINDOWS = (2, 4, 8, 16)
POOL_GROUP = 128
POOL_WIDTH = POOL_GROUP * len(POOL_WINDOWS)
POOL_HALO = max(POOL_WINDOWS)
ATTN_HEAD_DIM = 64
MOBA_BLOCK = 256
MOBA_TOPK = 3
XATTN_HEADS = 4
DEPTH = 1
DEEPNORM_ALPHA = (2.0 * DEPTH) ** 0.25
LN_EPS = 1e-5
LOG2E = math.log2(math.e)

V7X_LANES = 128
V7X_VMEM_BYTES = 64 * 1024 * 1024
HEADS_PER_SLAB = V7X_LANES // ATTN_HEAD_DIM

F32 = jnp.float32
BF16 = jnp.bfloat16
NEG_INF = float("-inf")


def _vmem_limit(resident_bytes):
    return int(min(2 * resident_bytes, V7X_VMEM_BYTES * 7 // 8))


def _const_spec(shape):
    return pl.BlockSpec(shape, lambda *_: (0,) * len(shape), pipeline_mode=pl.Buffered(1))


def _dot(a, b):
    return jnp.dot(a, b, preferred_element_type=F32)


def _layer_norm(y, g, b):
    mu = jnp.mean(y, axis=-1, keepdims=True)
    d = y - mu
    var = jnp.mean(d * d, axis=-1, keepdims=True)
    return d * lax.rsqrt(var + LN_EPS) * g + b


def _proj_pool_kernel(x_ref, w_in_ref, w_pool_ref, scale_ref,
                      pool_ref, q_ref, k_ref, v_ref, ubuf_ref, *, q_scale):
    s_idx = pl.program_id(1)
    ts = x_ref.shape[1]
    aw = q_ref.shape[2]
    xb = x_ref[0].astype(BF16)

    u = _dot(xb, w_in_ref[:, 0:POOL_WIDTH])
    q_ref[0] = (_dot(xb, w_in_ref[:, POOL_WIDTH:POOL_WIDTH + aw]) * q_scale).astype(BF16)
    k_ref[0] = _dot(xb, w_in_ref[:, POOL_WIDTH + aw:POOL_WIDTH + 2 * aw]).astype(BF16)
    v_ref[0] = _dot(xb, w_in_ref[:, POOL_WIDTH + 2 * aw:POOL_WIDTH + 3 * aw]).astype(BF16)

    @pl.when(s_idx == 0)
    def _():
        ubuf_ref[0:POOL_HALO, :] = jnp.zeros((POOL_HALO, POOL_WIDTH), F32)

    ubuf_ref[POOL_HALO:POOL_HALO + ts, :] = u
    pos = s_idx * ts + lax.broadcasted_iota(jnp.int32, (ts, POOL_GROUP), 0)
    for g, w in enumerate(POOL_WINDOWS):
        cols = slice(g * POOL_GROUP, (g + 1) * POOL_GROUP)
        win = ubuf_ref[:, cols]
        span = 1
        while span < w:
            win = win + pltpu.roll(win, span, axis=0)
            span *= 2
        inv_count = 1.0 / jnp.minimum(pos + 1, w).astype(F32)
        pooled = win[POOL_HALO:, :] * inv_count - u[:, cols]
        mixed = _dot(pooled.astype(BF16), w_pool_ref[g]) * scale_ref[:, cols]
        pool_ref[0, :, cols] = mixed.astype(BF16)
    ubuf_ref[0:POOL_HALO, :] = u[ts - POOL_HALO:, :]


def _proj_pool(x, w_in, w_pool, pool_scale, *, ts):
    B, S, D = x.shape
    aw = (w_in.shape[1] - POOL_WIDTH) // 3
    q_scale = ATTN_HEAD_DIM ** -0.5 * LOG2E
    slab = lambda width: pl.BlockSpec((1, ts, width), lambda b, s: (b, s, 0))
    resident = (2 * ts * D * 4 + D * w_in.shape[1] * 2 + 2 * ts * (POOL_WIDTH + 3 * aw) * 2
                + (ts + POOL_HALO) * POOL_WIDTH * 4 + 4 * ts * POOL_WIDTH * 4)
    return pl.pallas_call(
        functools.partial(_proj_pool_kernel, q_scale=q_scale),
        grid=(B, S // ts),
        in_specs=[slab(D), _const_spec(w_in.shape), _const_spec(w_pool.shape),
                  _const_spec(pool_scale.shape)],
        out_specs=[slab(POOL_WIDTH), slab(aw), slab(aw), slab(aw)],
        out_shape=[jax.ShapeDtypeStruct((B, S, POOL_WIDTH), BF16)]
        + [jax.ShapeDtypeStruct((B, S, aw), BF16)] * 3,
        scratch_shapes=[pltpu.VMEM((ts + POOL_HALO, POOL_WIDTH), F32)],
        compiler_params=pltpu.CompilerParams(
            dimension_semantics=("parallel", "arbitrary"),
            vmem_limit_bytes=_vmem_limit(resident)),
    )(x, w_in, w_pool, pool_scale)


def _moba_kernel(q_ref, k_ref, v_ref, o_ref,
                 qt_ref, vt_ref, kmean_ref, bias_ref, s_ref, m_ref, l_ref, acc_ref):
    S = q_ref.shape[1]
    nb = S // MOBA_BLOCK
    blk = MOBA_BLOCK
    hd = ATTN_HEAD_DIM
    heads = range(HEADS_PER_SLAB)

    chan_head = lax.broadcasted_iota(jnp.int32, (V7X_LANES, blk), 0) // hd
    for c in range(nb):
        cols = slice(c * blk, (c + 1) * blk)
        qtf = q_ref[0, cols, :].astype(F32).T
        for h in heads:
            qt_ref[h, :, cols] = jnp.where(chan_head == h, qtf, 0.0).astype(BF16)
        vt_ref[:, cols] = v_ref[0, cols, :].astype(F32).T.astype(BF16)
        kmean_ref[c:c + 1, :] = jnp.mean(k_ref[0, cols, :].astype(F32), axis=0, keepdims=True)

    blk_idx = lax.broadcasted_iota(jnp.int32, (nb, blk), 0)
    causal = (lax.broadcasted_iota(jnp.int32, (blk, blk), 0)
              <= lax.broadcasted_iota(jnp.int32, (blk, blk), 1))

    def scores(h, q0, k0):
        return _dot(k_ref[0, pl.ds(k0, blk), :], qt_ref[h, :, pl.ds(q0, blk)])

    def pair(i, carry):
        q_blocks = (i, nb - 1 - i)
        q_off = [pl.multiple_of(qb * blk, blk) for qb in q_blocks]

        def past_tile(t):
            first = t <= i
            sel = jnp.where(first, 0, 1)
            j = jnp.where(first, i - t, t - i - 1)
            q0 = pl.multiple_of(jnp.where(first, q_off[0], q_off[1]), blk)
            return sel, q0, j, pl.multiple_of(j * blk, blk)

        for sel in range(2):
            for h in heads:
                qhf = qt_ref[h, :, pl.ds(q_off[sel], blk)].astype(F32)
                gate = jnp.dot(kmean_ref[...], qhf, preferred_element_type=F32,
                               precision=lax.Precision.HIGHEST)
                past = blk_idx < q_blocks[sel]
                gate = jnp.where(past, gate, NEG_INF)
                rank = jnp.zeros((nb, blk), jnp.int32)
                for jp in range(nb):
                    other = gate[jp:jp + 1, :]
                    ahead = (other > gate) | ((other == gate) & (jp < blk_idx))
                    rank = rank + ahead.astype(jnp.int32)
                chosen = (rank < MOBA_TOPK) & past
                bias_ref[sel, h] = jnp.where(chosen, 0.0, NEG_INF)
        acc_ref[...] = jnp.zeros(acc_ref.shape, F32)
        l_ref[...] = jnp.zeros(l_ref.shape, F32)

        for sel, t in ((0, 0), (1, nb)):
            for h in heads:
                s = jnp.where(causal, scores(h, q_off[sel], q_off[sel]), NEG_INF)
                s_ref[h, t] = s
                m_ref[sel, h] = jnp.max(s, axis=0, keepdims=True)
        for t in range(1, nb):
            sel, q0, j, k0 = past_tile(t)
            for h in heads:
                s = scores(h, q0, k0)
                s_ref[h, t] = s
                tile_max = jnp.max(s, axis=0, keepdims=True) + bias_ref[sel, h, pl.ds(j, 1), :]
                m_ref[sel, h] = jnp.maximum(m_ref[sel, h], tile_max)

        def accumulate(t, h, sel, k0, m_eff):
            p = jnp.exp2(s_ref[h, t] - m_eff)
            l_ref[sel, h] += jnp.sum(p, axis=0, keepdims=True)
            acc_ref[sel, h] += _dot(vt_ref[h * hd:(h + 1) * hd, pl.ds(k0, blk)], p.astype(BF16))

        for sel, t in ((0, 0), (1, nb)):
            for h in heads:
                accumulate(t, h, sel, q_off[sel], m_ref[sel, h])
        for t in range(1, nb):
            sel, q0, j, k0 = past_tile(t)
            for h in heads:
                accumulate(t, h, sel, k0, m_ref[sel, h] - bias_ref[sel, h, pl.ds(j, 1), :])

        for sel in range(2):
            o_t = jnp.concatenate(
                [acc_ref[sel, h] * (1.0 / l_ref[sel, h]) for h in heads], axis=0)
            o_ref[0, pl.ds(q_off[sel], blk), :] = o_t.T.astype(BF16)
        return carry

    lax.fori_loop(0, nb // 2, pair, 0)


def _moba(q, k, v):
    B, S, aw = q.shape
    nb = S // MOBA_BLOCK
    blk, hps, hd = MOBA_BLOCK, HEADS_PER_SLAB, ATTN_HEAD_DIM
    slab = pl.BlockSpec((1, S, V7X_LANES), lambda b, g: (b, 0, g))
    scratch = [
        pltpu.VMEM((hps, V7X_LANES, S), BF16),
        pltpu.VMEM((V7X_LANES, S), BF16),
        pltpu.VMEM((nb, V7X_LANES), F32),
        pltpu.VMEM((2, hps, nb, blk), F32),
        pltpu.VMEM((hps, nb + 1, blk, blk), F32),
        pltpu.VMEM((2, hps, 1, blk), F32),
        pltpu.VMEM((2, hps, 1, blk), F32),
        pltpu.VMEM((2, hps, hd, blk), F32),
    ]
    resident = (8 * S * V7X_LANES * 2 + (hps + 1) * S * V7X_LANES * 2
                + hps * (nb + 1) * blk * blk * 4 + 4 * blk * blk * 4)
    return pl.pallas_call(
        _moba_kernel,
        grid=(B, aw // V7X_LANES),
        in_specs=[slab, slab, slab],
        out_specs=slab,
        out_shape=jax.ShapeDtypeStruct((B, S, aw), BF16),
        scratch_shapes=scratch,
        compiler_params=pltpu.CompilerParams(
            dimension_semantics=("parallel", "parallel"),
            vmem_limit_bytes=_vmem_limit(resident)),
    )(q, k, v)


def _mem_kv_kernel(mem_ref, w_ref, k_ref, v_ref):
    D = mem_ref.shape[2]
    mb = mem_ref[0].astype(BF16)
    k_ref[0] = _dot(mb, w_ref[:, 0:D]).astype(BF16)
    v_ref[0] = _dot(mb, w_ref[:, D:2 * D]).astype(BF16)


def _mem_kv(mem, w_xkv):
    B, M, D = mem.shape
    blk = pl.BlockSpec((1, M, D), lambda b: (b, 0, 0))
    resident = 2 * M * D * 4 + D * 2 * D * 2 + 4 * M * D * 2 + 2 * M * D * 4
    return pl.pallas_call(
        _mem_kv_kernel,
        grid=(B,),
        in_specs=[blk, _const_spec(w_xkv.shape)],
        out_specs=[blk, blk],
        out_shape=[jax.ShapeDtypeStruct((B, M, D), BF16)] * 2,
        compiler_params=pltpu.CompilerParams(
            dimension_semantics=("parallel",), vmem_limit_bytes=_vmem_limit(resident)),
    )(mem, w_xkv)


def _mix_xattn_kernel(x_ref, pool_ref, attn_ref, w_out_ref, g1_ref, b1_ref,
                      w_xq_ref, kx_ref, vx_ref, w_xo_ref, g2_ref, b2_ref, h_ref, *, q_scale):
    D = x_ref.shape[2]
    pw = pool_ref.shape[2]
    xhd = D // XATTN_HEADS
    mix = _dot(pool_ref[0], w_out_ref[0:pw, :]) + _dot(attn_ref[0], w_out_ref[pw:D, :])
    h1 = _layer_norm(DEEPNORM_ALPHA * x_ref[0] + mix, g1_ref[...], b1_ref[...])

    qx = (_dot(h1.astype(BF16), w_xq_ref[...]) * q_scale).astype(BF16)
    heads = []
    for h in range(XATTN_HEADS):
        cols = slice(h * xhd, (h + 1) * xhd)
        s = lax.dot_general(qx[:, cols], kx_ref[0, :, cols], (((1,), (1,)), ((), ())),
                            preferred_element_type=F32)
        p = jnp.exp2(s - jnp.max(s, axis=-1, keepdims=True))
        inv_l = 1.0 / jnp.sum(p, axis=-1, keepdims=True)
        heads.append((_dot(p.astype(BF16), vx_ref[0, :, cols]) * inv_l).astype(BF16))
    xo = _dot(jnp.concatenate(heads, axis=-1), w_xo_ref[...])
    h_ref[0] = _layer_norm(DEEPNORM_ALPHA * h1 + xo, g2_ref[...], b2_ref[...])


def _mix_xattn(x, pool, attn, w_out, g1, b1, w_xq, kx, vx, w_xo, g2, b2, *, ts):
    B, S, D = x.shape
    M = kx.shape[1]
    q_scale = (D // XATTN_HEADS) ** -0.5 * LOG2E
    tile = lambda width: pl.BlockSpec((1, ts, width), lambda b, s: (b, s, 0))
    mem_blk = pl.BlockSpec((1, M, D), lambda b, s: (b, 0, 0))
    vec = _const_spec((1, D))
    resident = (4 * ts * D * 4 + 4 * ts * pool.shape[2] * 2 + 3 * D * D * 2 + 4 * M * D * 2
                + 6 * ts * D * 4)
    return pl.pallas_call(
        functools.partial(_mix_xattn_kernel, q_scale=q_scale),
        grid=(B, S // ts),
        in_specs=[tile(D), tile(pool.shape[2]), tile(attn.shape[2]), _const_spec(w_out.shape),
                  vec, vec, _const_spec(w_xq.shape), mem_blk, mem_blk, _const_spec(w_xo.shape),
                  vec, vec],
        out_specs=tile(D),
        out_shape=jax.ShapeDtypeStruct((B, S, D), F32),
        compiler_params=pltpu.CompilerParams(
            dimension_semantics=("parallel", "parallel"),
            vmem_limit_bytes=_vmem_limit(resident)),
    )(x, pool, attn, w_out, g1, b1, w_xq, kx, vx, w_xo, g2, b2)


def _ffn_kernel(h_ref, w_gate_ref, w_up_ref, w_down_ref, g_ref, b_ref, o_ref, *, ff_chunk):
    h = h_ref[...]
    hb = h.astype(BF16)
    d_ff = w_gate_ref.shape[1]
    y = jnp.zeros(h.shape, F32)
    for c in range(d_ff // ff_chunk):
        cols = slice(c * ff_chunk, (c + 1) * ff_chunk)
        gate = _dot(hb, w_gate_ref[:, cols])
        up = _dot(hb, w_up_ref[:, cols])
        act = gate * (1.0 / (1.0 + jnp.exp(-gate))) * up
        y = y + _dot(act.astype(BF16), w_down_ref[cols, :])
    o_ref[...] = _layer_norm(DEEPNORM_ALPHA * h + y, g_ref[...], b_ref[...])


def _ffn(h, w_gate, w_up, w_down, g, b, *, ts, ff_chunk):
    N, D = h.shape
    d_ff = w_gate.shape[1]
    tile = pl.BlockSpec((ts, D), lambda t: (t, 0))
    vec = _const_spec((1, D))
    resident = 4 * ts * D * 4 + 3 * D * d_ff * 2 + 3 * ts * ff_chunk * 4 + 2 * ts * D * 4
    return pl.pallas_call(
        functools.partial(_ffn_kernel, ff_chunk=ff_chunk),
        grid=(N // ts,),
        in_specs=[tile, _const_spec(w_gate.shape), _const_spec(w_up.shape),
                  _const_spec(w_down.shape), vec, vec],
        out_specs=tile,
        out_shape=jax.ShapeDtypeStruct((N, D), F32),
        compiler_params=pltpu.CompilerParams(
            dimension_semantics=("parallel",), vmem_limit_bytes=_vmem_limit(resident)),
    )(h, w_gate, w_up, w_down, g, b)


def kernel(x, mem, w_in, w_pool, pool_scale, w_out, ln1_g, ln1_b, w_xq, w_xkv, w_xo,
           ln2_g, ln2_b, w_gate, w_up, w_down, ln3_g, ln3_b):
    B, S, D = x.shape
    assert w_in.shape[0] == DEPTH and S % (2 * MOBA_BLOCK) == 0
    assert (w_in.shape[2] - POOL_WIDTH) % (3 * V7X_LANES) == 0
    h = x
    for l in range(DEPTH):
        wb = lambda w: w[l].astype(BF16)
        row = lambda p: p[l].reshape(1, -1)
        pool, q, k, v = _proj_pool(h, wb(w_in), wb(w_pool), row(pool_scale), ts=1024)
        attn = _moba(q, k, v)
        kx, vx = _mem_kv(mem, wb(w_xkv))
        h = _mix_xattn(h, pool, attn, wb(w_out), row(ln1_g), row(ln1_b), wb(w_xq), kx, vx,
                       wb(w_xo), row(ln2_g), row(ln2_b), ts=512)
        h = _ffn(h.reshape(B * S, D), wb(w_gate), wb(w_up), wb(w_down),
                 row(ln3_g), row(ln3_b), ts=512, ff_chunk=256).reshape(B, S, D)
    return h
```

```python
import functools
import math

import jax
import jax.numpy as jnp
from jax import lax
from jax.experimental import pallas as pl
from jax.experimental.pallas import tpu as pltpu

POOL_WINDOWS = (2, 4, 8, 16)
POOL_GROUP = 128
POOL_WIDTH = POOL_GROUP * len(POOL_WINDOWS)
POOL_HALO = max(POOL_WINDOWS)
ATTN_HEAD_DIM = 64
MOBA_BLOCK = 256
MOBA_TOPK = 3
XATTN_HEADS = 4
DEPTH = 1
DEEPNORM_ALPHA = (2.0 * DEPTH) ** 0.25
LN_EPS = 1e-5
LOG2E = math.log2(math.e)

V7X_LANES = 128
V7X_VMEM_BYTES = 64 * 1024 * 1024
V7X_BF16_ROWS = 16
HEADS_PER_SLAB = V7X_LANES // ATTN_HEAD_DIM

F32 = jnp.float32
BF16 = jnp.bfloat16
NEG_INF = float("-inf")


def _vmem_limit(resident_bytes):
    return int(min(2 * resident_bytes, V7X_VMEM_BYTES * 7 // 8))


def _const_spec(shape):
    return pl.BlockSpec(shape, lambda *_: (0,) * len(shape), pipeline_mode=pl.Buffered(1))


def _dot(a, b):
    return jnp.dot(a, b, preferred_element_type=F32)


def _layer_norm(y, g, b):
    mu = jnp.mean(y, axis=-1, keepdims=True)
    d = y - mu
    var = jnp.mean(d * d, axis=-1, keepdims=True)
    return d * lax.rsqrt(var + LN_EPS) * g + b


def _proj_pool_kernel(x_ref, w_in_ref, w_pool_ref, scale_ref,
                      pool_ref, q_ref, k_ref, v_ref, ubuf_ref, *, q_scale):
    s_idx = pl.program_id(1)
    ts = x_ref.shape[1]
    aw = q_ref.shape[2]
    xb = x_ref[0].astype(BF16)

    u = _dot(xb, w_in_ref[:, 0:POOL_WIDTH])
    q_ref[0] = (_dot(xb, w_in_ref[:, POOL_WIDTH:POOL_WIDTH + aw]) * q_scale).astype(BF16)
    k_ref[0] = _dot(xb, w_in_ref[:, POOL_WIDTH + aw:POOL_WIDTH + 2 * aw]).astype(BF16)
    v_ref[0] = _dot(xb, w_in_ref[:, POOL_WIDTH + 2 * aw:POOL_WIDTH + 3 * aw]).astype(BF16)

    @pl.when(s_idx == 0)
    def _():
        ubuf_ref[0:POOL_HALO, :] = jnp.zeros((POOL_HALO, POOL_WIDTH), F32)

    ubuf_ref[POOL_HALO:POOL_HALO + ts, :] = u
    pos = s_idx * ts + lax.broadcasted_iota(jnp.int32, (ts, POOL_GROUP), 0)
    for g, w in enumerate(POOL_WINDOWS):
        cols = slice(g * POOL_GROUP, (g + 1) * POOL_GROUP)
        win = ubuf_ref[:, cols]
        span = 1
        while span < w:
            win = win + pltpu.roll(win, span, axis=0)
            span *= 2
        inv_count = 1.0 / jnp.minimum(pos + 1, w).astype(F32)
        pooled = win[POOL_HALO:, :] * inv_count - u[:, cols]
        mixed = _dot(pooled.astype(BF16), w_pool_ref[g]) * scale_ref[:, cols]
        pool_ref[0, :, cols] = mixed.astype(BF16)
    ubuf_ref[0:POOL_HALO, :] = u[ts - POOL_HALO:, :]


def _proj_pool(x, w_in, w_pool, pool_scale, *, ts):
    B, S, D = x.shape
    aw = (w_in.shape[1] - POOL_WIDTH) // 3
    q_scale = ATTN_HEAD_DIM ** -0.5 * LOG2E
    slab = lambda width: pl.BlockSpec((1, ts, width), lambda b, s: (b, s, 0))
    resident = (2 * ts * D * 4 + D * w_in.shape[1] * 2 + 2 * ts * (POOL_WIDTH + 3 * aw) * 2
                + (ts + POOL_HALO) * POOL_WIDTH * 4 + 4 * ts * POOL_WIDTH * 4)
    return pl.pallas_call(
        functools.partial(_proj_pool_kernel, q_scale=q_scale),
        grid=(B, S // ts),
        in_specs=[slab(D), _const_spec(w_in.shape), _const_spec(w_pool.shape),
                  _const_spec(pool_scale.shape)],
        out_specs=[slab(POOL_WIDTH), slab(aw), slab(aw), slab(aw)],
        out_shape=[jax.ShapeDtypeStruct((B, S, POOL_WIDTH), BF16)]
        + [jax.ShapeDtypeStruct((B, S, aw), BF16)] * 3,
        scratch_shapes=[pltpu.VMEM((ts + POOL_HALO, POOL_WIDTH), F32)],
        compiler_params=pltpu.CompilerParams(
            dimension_semantics=("parallel", "arbitrary"),
            vmem_limit_bytes=_vmem_limit(resident)),
    )(x, w_in, w_pool, pool_scale)


def _moba_kernel(q_ref, k_ref, v_ref, o_ref, qt_ref, vt_ref, kmean_ref, acc_ref,
                 s0_ref, s1_ref, p0_ref, p1_ref, bias0_ref, bias1_ref,
                 m0_ref, m1_ref, tmax0_ref, tmax1_ref):
    S = q_ref.shape[1]
    nb = S // MOBA_BLOCK
    blk = MOBA_BLOCK
    hd = ATTN_HEAD_DIM
    heads = range(HEADS_PER_SLAB)
    n_pairs = nb // 2
    s_refs, p_refs = (s0_ref, s1_ref), (p0_ref, p1_ref)
    bias_refs, m_refs = (bias0_ref, bias1_ref), (m0_ref, m1_ref)
    tmax_refs = (tmax0_ref, tmax1_ref)

    chan_head = lax.broadcasted_iota(jnp.int32, (V7X_LANES, blk), 0) // hd
    vrows = vt_ref.shape[1]
    ones_row = jnp.where(lax.broadcasted_iota(jnp.int32, (vrows - hd, blk), 0) == 0,
                         1.0, 0.0).astype(BF16)
    for c in range(nb):
        cols = slice(c * blk, (c + 1) * blk)
        qtf = q_ref[0, cols, :].astype(F32).T
        for h in heads:
            qt_ref[h, :, cols] = jnp.where(chan_head == h, qtf, 0.0).astype(BF16)
        vtb = v_ref[0, cols, :].astype(F32).T.astype(BF16)
        for h in heads:
            vt_ref[h, 0:hd, cols] = vtb[h * hd:(h + 1) * hd, :]
            vt_ref[h, hd:vrows, cols] = ones_row
        kmean_ref[c:c + 1, :] = jnp.mean(k_ref[0, cols, :].astype(F32), axis=0, keepdims=True)

    blk_idx = lax.broadcasted_iota(jnp.int32, (nb, blk), 0)
    causal = (lax.broadcasted_iota(jnp.int32, (blk, blk), 0)
              <= lax.broadcasted_iota(jnp.int32, (blk, blk), 1))

    def q_offsets(i):
        return [pl.multiple_of(qb * blk, blk) for qb in (i, nb - 1 - i)]

    def past_tile(i, t):
        first = t <= i
        sel = jnp.where(first, 0, 1)
        j = jnp.where(first, i - t, t - i - 1)
        q_off = q_offsets(i)
        q0 = pl.multiple_of(jnp.where(first, q_off[0], q_off[1]), blk)
        return sel, q0, j, pl.multiple_of(j * blk, blk)

    def gates(i, bias_ref):
        for sel, qb in ((0, i), (1, nb - 1 - i)):
            q0 = pl.multiple_of(qb * blk, blk)
            for h in heads:
                qhf = qt_ref[h, :, pl.ds(q0, blk)].astype(F32)
                gate = jnp.dot(kmean_ref[...], qhf, preferred_element_type=F32,
                               precision=lax.Precision.HIGHEST)
                past = blk_idx < qb
                gate = jnp.where(past, gate, NEG_INF)
                rank = jnp.zeros((nb, blk), jnp.int32)
                for jp in range(nb):
                    other = gate[jp:jp + 1, :]
                    ahead = (other > gate) | ((other == gate) & (jp < blk_idx))
                    rank = rank + ahead.astype(jnp.int32)
                chosen = (rank < MOBA_TOPK) & past
                bias_ref[sel, h] = jnp.where(chosen, 0.0, NEG_INF)

    def score_tile(i, t, h, s_ref, tmax_ref):
        q_off = q_offsets(i)
        if t in (0, nb):
            q0 = k0 = q_off[0 if t == 0 else 1]
        else:
            _, q0, _, k0 = past_tile(i, t)
        s = _dot(k_ref[0, pl.ds(k0, blk), :], qt_ref[h, :, pl.ds(q0, blk)])
        if t in (0, nb):
            s = jnp.where(causal, s, NEG_INF)
        s_ref[h, t] = s
        tmax_ref[h, t] = jnp.max(s, axis=0, keepdims=True)

    def chosen_max(i, tmax_ref, bias_ref, m_ref):
        for h in heads:
            m = [tmax_ref[h, 0], tmax_ref[h, nb]]
            for t in range(1, nb):
                sel, _, j, _ = past_tile(i, t)
                tm = tmax_ref[h, t] + bias_ref[sel, h, pl.ds(j, 1), :]
                other = jnp.where(t <= i, NEG_INF, 0.0)
                mine = jnp.where(t <= i, 0.0, NEG_INF)
                m = [jnp.maximum(m[0], tm + mine), jnp.maximum(m[1], tm + other)]
            m_ref[0, h] = m[0]
            m_ref[1, h] = m[1]

    def prob_tile(i, t, h, s_ref, m_ref, bias_ref, p_ref):
        if t in (0, nb):
            sel = 0 if t == 0 else 1
            m_eff = m_ref[sel, h]
        else:
            sel, _, j, _ = past_tile(i, t)
            m_eff = m_ref[sel, h] - bias_ref[sel, h, pl.ds(j, 1), :]
        p_ref[h, t] = jnp.exp2(s_ref[h, t] - m_eff).astype(BF16)

    def value_tile(i, t, h, p_ref):
        if t in (0, nb):
            sel = 0 if t == 0 else 1
            k0 = q_offsets(i)[sel]
        else:
            sel, _, _, k0 = past_tile(i, t)
        acc_ref[sel, h] += _dot(vt_ref[h, :, pl.ds(k0, blk)], p_ref[h, t])

    def store(i):
        q_off = q_offsets(i)
        for sel in range(2):
            o_t = jnp.concatenate(
                [acc_ref[sel, h, 0:hd, :] * (1.0 / acc_ref[sel, h, hd:hd + 1, :]) for h in heads],
                axis=0)
            o_ref[0, pl.ds(q_off[sel], blk), :] = o_t.T.astype(BF16)

    tile_order = [0, nb] + list(range(1, nb))

    def pipeline_step(i, par, with_scores=True, with_values=True):
        cur, nxt = par, 1 - par
        chosen_max(i, tmax_refs[cur], bias_refs[cur], m_refs[cur])
        if with_values:
            acc_ref[...] = jnp.zeros(acc_ref.shape, F32)
        for n, t in enumerate(tile_order):
            for h in heads:
                if with_scores:
                    score_tile(i + 1, t, h, s_refs[nxt], tmax_refs[nxt])
                prob_tile(i, t, h, s_refs[cur], m_refs[cur], bias_refs[cur], p_refs[cur])
                if with_values:
                    value_tile(i - 1, t, h, p_refs[nxt])
            if with_scores and n == len(tile_order) // 2:
                gates(i + 1, bias_refs[nxt])
        if with_values:
            store(i - 1)

    gates(0, bias_refs[0])
    for t in tile_order:
        for h in heads:
            score_tile(0, t, h, s_refs[0], tmax_refs[0])
    pipeline_step(0, 0, with_values=False)

    def two_pairs(d, carry):
        pipeline_step(2 * d + 1, 1)
        pipeline_step(2 * d + 2, 0)
        return carry

    lax.fori_loop(0, n_pairs // 2 - 1, two_pairs, 0)
    last = n_pairs - 1
    pipeline_step(last, last % 2, with_scores=False)
    acc_ref[...] = jnp.zeros(acc_ref.shape, F32)
    for t in tile_order:
        for h in heads:
            value_tile(last, t, h, p_refs[last % 2])
    store(last)


def _moba(q, k, v):
    B, S, aw = q.shape
    nb = S // MOBA_BLOCK
    blk, hps, hd = MOBA_BLOCK, HEADS_PER_SLAB, ATTN_HEAD_DIM
    assert nb % 4 == 0
    slab = pl.BlockSpec((1, S, V7X_LANES), lambda b, g: (b, 0, g))
    per_parity = [
        pltpu.VMEM((hps, nb + 1, blk, blk), F32),
        pltpu.VMEM((hps, nb + 1, blk, blk), BF16),
        pltpu.VMEM((2, hps, nb, blk), F32),
        pltpu.VMEM((2, hps, 1, blk), F32),
        pltpu.VMEM((hps, nb + 1, 1, blk), F32),
    ]
    scratch = [
        pltpu.VMEM((hps, V7X_LANES, S), BF16),
        pltpu.VMEM((hps, hd + V7X_BF16_ROWS, S), BF16),
        pltpu.VMEM((nb, V7X_LANES), F32),
        pltpu.VMEM((2, hps, hd + V7X_BF16_ROWS, blk), F32),
    ] + [spec for spec in per_parity for _ in range(2)]
    resident = (8 * S * V7X_LANES * 2 + (hps + 1) * S * V7X_LANES * 2
                + 2 * hps * (nb + 1) * blk * blk * 6 + 4 * blk * blk * 4)
    return pl.pallas_call(
        _moba_kernel,
        grid=(B, aw // V7X_LANES),
        in_specs=[slab, slab, slab],
        out_specs=slab,
        out_shape=jax.ShapeDtypeStruct((B, S, aw), BF16),
        scratch_shapes=scratch,
        compiler_params=pltpu.CompilerParams(
            dimension_semantics=("parallel", "parallel"),
            vmem_limit_bytes=_vmem_limit(resident)),
    )(q, k, v)


def _mem_kv_kernel(mem_ref, w_ref, k_ref, v_ref):
    D = mem_ref.shape[2]
    mb = mem_ref[0].astype(BF16)
    k_ref[0] = _dot(mb, w_ref[:, 0:D]).astype(BF16)
    v_ref[0] = _dot(mb, w_ref[:, D:2 * D]).astype(BF16)


def _mem_kv(mem, w_xkv):
    B, M, D = mem.shape
    blk = pl.BlockSpec((1, M, D), lambda b: (b, 0, 0))
    resident = 2 * M * D * 4 + D * 2 * D * 2 + 4 * M * D * 2 + 2 * M * D * 4
    return pl.pallas_call(
        _mem_kv_kernel,
        grid=(B,),
        in_specs=[blk, _const_spec(w_xkv.shape)],
        out_specs=[blk, blk],
        out_shape=[jax.ShapeDtypeStruct((B, M, D), BF16)] * 2,
        compiler_params=pltpu.CompilerParams(
            dimension_semantics=("parallel",), vmem_limit_bytes=_vmem_limit(resident)),
    )(mem, w_xkv)


def _mix_xattn_kernel(x_ref, pool_ref, attn_ref, w_out_ref, g1_ref, b1_ref,
                      w_xq_ref, kx_ref, vx_ref, w_xo_ref, g2_ref, b2_ref, h_ref, *, q_scale):
    D = x_ref.shape[2]
    pw = pool_ref.shape[2]
    xhd = D // XATTN_HEADS
    mix = _dot(pool_ref[0], w_out_ref[0:pw, :]) + _dot(attn_ref[0], w_out_ref[pw:D, :])
    h1 = _layer_norm(DEEPNORM_ALPHA * x_ref[0] + mix, g1_ref[...], b1_ref[...])

    qx = (_dot(h1.astype(BF16), w_xq_ref[...]) * q_scale).astype(BF16)
    heads = []
    for h in range(XATTN_HEADS):
        cols = slice(h * xhd, (h + 1) * xhd)
        s = lax.dot_general(qx[:, cols], kx_ref[0, :, cols], (((1,), (1,)), ((), ())),
                            preferred_element_type=F32)
        p = jnp.exp2(s - jnp.max(s, axis=-1, keepdims=True))
        inv_l = 1.0 / jnp.sum(p, axis=-1, keepdims=True)
        heads.append((_dot(p.astype(BF16), vx_ref[0, :, cols]) * inv_l).astype(BF16))
    xo = _dot(jnp.concatenate(heads, axis=-1), w_xo_ref[...])
    h_ref[0] = _layer_norm(DEEPNORM_ALPHA * h1 + xo, g2_ref[...], b2_ref[...])


def _mix_xattn(x, pool, attn, w_out, g1, b1, w_xq, kx, vx, w_xo, g2, b2, *, ts):
    B, S, D = x.shape
    M = kx.shape[1]
    q_scale = (D // XATTN_HEADS) ** -0.5 * LOG2E
    tile = lambda width: pl.BlockSpec((1, ts, width), lambda b, s: (b, s, 0))
    mem_blk = pl.BlockSpec((1, M, D), lambda b, s: (b, 0, 0))
    vec = _const_spec((1, D))
    resident = (4 * ts * D * 4 + 4 * ts * pool.shape[2] * 2 + 3 * D * D * 2 + 4 * M * D * 2
                + 6 * ts * D * 4)
    return pl.pallas_call(
        functools.partial(_mix_xattn_kernel, q_scale=q_scale),
        grid=(B, S // ts),
        in_specs=[tile(D), tile(pool.shape[2]), tile(attn.shape[2]), _const_spec(w_out.shape),
                  vec, vec, _const_spec(w_xq.shape), mem_blk, mem_blk, _const_spec(w_xo.shape),
                  vec, vec],
        out_specs=tile(D),
        out_shape=jax.ShapeDtypeStruct((B, S, D), F32),
        compiler_params=pltpu.CompilerParams(
            dimension_semantics=("parallel", "parallel"),
            vmem_limit_bytes=_vmem_limit(resident)),
    )(x, pool, attn, w_out, g1, b1, w_xq, kx, vx, w_xo, g2, b2)


def _ffn_kernel(h_ref, w_gate_ref, w_up_ref, w_down_ref, g_ref, b_ref, o_ref, *, ff_chunk):
    h = h_ref[...]
    hb = h.astype(BF16)
    d_ff = w_gate_ref.shape[1]
    y = jnp.zeros(h.shape, F32)
    for c in range(d_ff // ff_chunk):
        cols = slice(c * ff_chunk, (c + 1) * ff_chunk)
        gate = _dot(hb, w_gate_ref[:, cols])
        up = _dot(hb, w_up_ref[:, cols])
        act = gate * (1.0 / (1.0 + jnp.exp(-gate))) * up
        y = y + _dot(act.astype(BF16), w_down_ref[cols, :])
    o_ref[...] = _layer_norm(DEEPNORM_ALPHA * h + y, g_ref[...], b_ref[...])


def _ffn(h, w_gate, w_up, w_down, g, b, *, ts, ff_chunk):
    N, D = h.shape
    d_ff = w_gate.shape[1]
    tile = pl.BlockSpec((ts, D), lambda t: (t, 0))
    vec = _const_spec((1, D))
    resident = 4 * ts * D * 4 + 3 * D * d_ff * 2 + 3 * ts * ff_chunk * 4 + 2 * ts * D * 4
    return pl.pallas_call(
        functools.partial(_ffn_kernel, ff_chunk=ff_chunk),
        grid=(N // ts,),
        in_specs=[tile, _const_spec(w_gate.shape), _const_spec(w_up.shape),
                  _const_spec(w_down.shape), vec, vec],
        out_specs=tile,
        out_shape=jax.ShapeDtypeStruct((N, D), F32),
        compiler_params=pltpu.CompilerParams(
            dimension_semantics=("parallel",), vmem_limit_bytes=_vmem_limit(resident)),
    )(h, w_gate, w_up, w_down, g, b)


def kernel(x, mem, w_in, w_pool, pool_scale, w_out, ln1_g, ln1_b, w_xq, w_xkv, w_xo,
           ln2_g, ln2_b, w_gate, w_up, w_down, ln3_g, ln3_b):
    B, S, D = x.shape
    assert w_in.shape[0] == DEPTH and S % (4 * MOBA_BLOCK) == 0
    assert (w_in.shape[2] - POOL_WIDTH) % (3 * V7X_LANES) == 0
    h = x
    for l in range(DEPTH):
        wb = lambda w: w[l].astype(BF16)
        row = lambda p: p[l].reshape(1, -1)
        pool, q, k, v = _proj_pool(h, wb(w_in), wb(w_pool), row(pool_scale), ts=1024)
        attn = _moba(q, k, v)
        kx, vx = _mem_kv(mem, wb(w_xkv))
        h = _mix_xattn(h, pool, attn, wb(w_out), row(ln1_g), row(ln1_b), wb(w_xq), kx, vx,
                       wb(w_xo), row(ln2_g), row(ln2_b), ts=512)
        h = _ffn(h.reshape(B * S, D), wb(w_gate), wb(w_up), wb(w_down),
                 row(ln3_g), row(ln3_b), ts=512, ff_chunk=256).reshape(B, S, D)
    return h
```

```python
import functools
import math

import jax
import jax.numpy as jnp
from jax import lax
from jax.experimental import pallas as pl
from jax.experimental.pallas import tpu as pltpu

POOL_WINDOWS = (2, 4, 8, 16)
POOL_GROUP = 128
POOL_WIDTH = POOL_GROUP * len(POOL_WINDOWS)
POOL_HALO = max(POOL_WINDOWS)
ATTN_HEAD_DIM = 64
MOBA_BLOCK = 256
MOBA_TOPK = 3
XATTN_HEADS = 4
DEPTH = 1
DEEPNORM_ALPHA = (2.0 * DEPTH) ** 0.25
LN_EPS = 1e-5
LOG2E = math.log2(math.e)

V7X_LANES = 128
V7X_SUBLANES = 8
V7X_VMEM_BYTES = 64 * 1024 * 1024
V7X_BF16_ROWS = 16
HEADS_PER_SLAB = V7X_LANES // ATTN_HEAD_DIM

F32 = jnp.float32
BF16 = jnp.bfloat16
NEG_INF = float("-inf")


def _vmem_limit(resident_bytes):
    return int(min(2 * resident_bytes, V7X_VMEM_BYTES * 7 // 8))


def _const_spec(shape):
    return pl.BlockSpec(shape, lambda *_: (0,) * len(shape), pipeline_mode=pl.Buffered(1))


def _dot(a, b):
    return jnp.dot(a, b, preferred_element_type=F32)


def _layer_norm(y, g, b):
    mu = jnp.mean(y, axis=-1, keepdims=True)
    d = y - mu
    var = jnp.mean(d * d, axis=-1, keepdims=True)
    return d * lax.rsqrt(var + LN_EPS) * g + b


def _proj_pool_kernel(x_ref, w_in_ref, w_pool_ref, scale_ref,
                      pool_ref, q_ref, k_ref, v_ref, ubuf_ref, *, q_scale):
    s_idx = pl.program_id(1)
    ts = x_ref.shape[1]
    aw = q_ref.shape[2]
    xb = x_ref[0].astype(BF16)

    u = _dot(xb, w_in_ref[:, 0:POOL_WIDTH])
    q_ref[0] = (_dot(xb, w_in_ref[:, POOL_WIDTH:POOL_WIDTH + aw]) * q_scale).astype(BF16)
    k_ref[0] = _dot(xb, w_in_ref[:, POOL_WIDTH + aw:POOL_WIDTH + 2 * aw]).astype(BF16)
    v_ref[0] = _dot(xb, w_in_ref[:, POOL_WIDTH + 2 * aw:POOL_WIDTH + 3 * aw]).astype(BF16)

    @pl.when(s_idx == 0)
    def _():
        ubuf_ref[0:POOL_HALO, :] = jnp.zeros((POOL_HALO, POOL_WIDTH), F32)

    ubuf_ref[POOL_HALO:POOL_HALO + ts, :] = u
    pos = s_idx * ts + lax.broadcasted_iota(jnp.int32, (ts, POOL_GROUP), 0)
    for g, w in enumerate(POOL_WINDOWS):
        cols = slice(g * POOL_GROUP, (g + 1) * POOL_GROUP)
        win = ubuf_ref[:, cols]
        span = 1
        while span < w:
            win = win + pltpu.roll(win, span, axis=0)
            span *= 2
        inv_count = 1.0 / jnp.minimum(pos + 1, w).astype(F32)
        pooled = win[POOL_HALO:, :] * inv_count - u[:, cols]
        mixed = _dot(pooled.astype(BF16), w_pool_ref[g]) * scale_ref[:, cols]
        pool_ref[0, :, cols] = mixed.astype(BF16)
    ubuf_ref[0:POOL_HALO, :] = u[ts - POOL_HALO:, :]


def _proj_pool(x, w_in, w_pool, pool_scale, *, ts):
    B, S, D = x.shape
    aw = (w_in.shape[1] - POOL_WIDTH) // 3
    q_scale = ATTN_HEAD_DIM ** -0.5 * LOG2E
    slab = lambda width: pl.BlockSpec((1, ts, width), lambda b, s: (b, s, 0))
    resident = (2 * ts * D * 4 + D * w_in.shape[1] * 2 + 2 * ts * (POOL_WIDTH + 3 * aw) * 2
                + (ts + POOL_HALO) * POOL_WIDTH * 4 + 4 * ts * POOL_WIDTH * 4)
    return pl.pallas_call(
        functools.partial(_proj_pool_kernel, q_scale=q_scale),
        grid=(B, S // ts),
        in_specs=[slab(D), _const_spec(w_in.shape), _const_spec(w_pool.shape),
                  _const_spec(pool_scale.shape)],
        out_specs=[slab(POOL_WIDTH), slab(aw), slab(aw), slab(aw)],
        out_shape=[jax.ShapeDtypeStruct((B, S, POOL_WIDTH), BF16)]
        + [jax.ShapeDtypeStruct((B, S, aw), BF16)] * 3,
        scratch_shapes=[pltpu.VMEM((ts + POOL_HALO, POOL_WIDTH), F32)],
        compiler_params=pltpu.CompilerParams(
            dimension_semantics=("parallel", "arbitrary"),
            vmem_limit_bytes=_vmem_limit(resident)),
    )(x, w_in, w_pool, pool_scale)


def _moba_kernel(q_ref, k_ref, v_ref, o_ref, qt_ref, vt_ref, kmean_ref, acc_ref,
                 s0_ref, s1_ref, p0_ref, p1_ref, bias0_ref, bias1_ref,
                 meff0_ref, meff1_ref, tmax0_ref, tmax1_ref):
    S = q_ref.shape[1]
    nb = S // MOBA_BLOCK
    blk = MOBA_BLOCK
    hd = ATTN_HEAD_DIM
    heads = range(HEADS_PER_SLAB)
    n_pairs = nb // 2
    s_refs, p_refs = (s0_ref, s1_ref), (p0_ref, p1_ref)
    bias_refs, meff_refs = (bias0_ref, bias1_ref), (meff0_ref, meff1_ref)
    tmax_refs = (tmax0_ref, tmax1_ref)
    trows = tmax0_ref.shape[1]

    chan_head = lax.broadcasted_iota(jnp.int32, (V7X_LANES, blk), 0) // hd
    vrows = vt_ref.shape[1]
    ones_row = jnp.where(lax.broadcasted_iota(jnp.int32, (vrows - hd, blk), 0) == 0,
                         1.0, 0.0).astype(BF16)
    for c in range(nb):
        cols = slice(c * blk, (c + 1) * blk)
        qtf = q_ref[0, cols, :].astype(F32).T
        for h in heads:
            qt_ref[h, :, cols] = jnp.where(chan_head == h, qtf, 0.0).astype(BF16)
        vtb = v_ref[0, cols, :].astype(F32).T.astype(BF16)
        for h in heads:
            vt_ref[h, 0:hd, cols] = vtb[h * hd:(h + 1) * hd, :]
            vt_ref[h, hd:vrows, cols] = ones_row
        kmean_ref[c:c + 1, :] = jnp.mean(k_ref[0, cols, :].astype(F32), axis=0, keepdims=True)

    blk_idx = lax.broadcasted_iota(jnp.int32, (nb, blk), 0)
    tile_idx = lax.broadcasted_iota(jnp.int32, (trows, blk), 0)
    for tmax_ref in tmax_refs:
        tmax_ref[...] = jnp.full(tmax_ref.shape, NEG_INF, F32)
    causal = (lax.broadcasted_iota(jnp.int32, (blk, blk), 0)
              <= lax.broadcasted_iota(jnp.int32, (blk, blk), 1))

    def q_offsets(i):
        return [pl.multiple_of(qb * blk, blk) for qb in (i, nb - 1 - i)]

    def past_tile(i, t):
        first = t <= i
        sel = jnp.where(first, 0, 1)
        j = jnp.where(first, t - 1, t - i - 1)
        q_off = q_offsets(i)
        q0 = pl.multiple_of(jnp.where(first, q_off[0], q_off[1]), blk)
        return sel, q0, j, pl.multiple_of(j * blk, blk)

    def gates(i, bias_ref):
        never = jnp.full((trows - nb, blk), NEG_INF, F32)
        for h in heads:
            by_block = []
            for sel, qb in ((0, i), (1, nb - 1 - i)):
                q0 = pl.multiple_of(qb * blk, blk)
                qhf = qt_ref[h, :, pl.ds(q0, blk)].astype(F32)
                gate = jnp.dot(kmean_ref[...], qhf, preferred_element_type=F32,
                               precision=lax.Precision.HIGHEST)
                past = blk_idx < qb
                gate = jnp.where(past, gate, NEG_INF)
                rank = jnp.zeros((nb, blk), jnp.int32)
                for jp in range(nb):
                    other = gate[jp:jp + 1, :]
                    ahead = (other > gate) | ((other == gate) & (jp < blk_idx))
                    rank = rank + ahead.astype(jnp.int32)
                chosen = (rank < MOBA_TOPK) & past
                by_block.append(jnp.concatenate([jnp.where(chosen, 0.0, NEG_INF), never], axis=0))
            first = pltpu.roll(by_block[0], 1, axis=0)
            second = pltpu.roll(by_block[1], i + 1, axis=0)
            own = (tile_idx == 0) | (tile_idx == nb)
            bias_ref[h] = jnp.where(own, 0.0, jnp.where(tile_idx <= i, first, second))

    def score_tile(i, t, h, s_ref, tmax_ref):
        q_off = q_offsets(i)
        if t in (0, nb):
            q0 = k0 = q_off[0 if t == 0 else 1]
        else:
            _, q0, _, k0 = past_tile(i, t)
        s = _dot(k_ref[0, pl.ds(k0, blk), :], qt_ref[h, :, pl.ds(q0, blk)])
        if t in (0, nb):
            s = jnp.where(causal, s, NEG_INF)
        s_ref[h, t] = s
        tmax_ref[h, t:t + 1, :] = jnp.max(s, axis=0, keepdims=True)

    def chosen_max(i, tmax_ref, bias_ref, meff_ref):
        in_first = tile_idx <= i
        in_second = jnp.logical_and(tile_idx > i, tile_idx <= nb)
        for h in heads:
            bias = bias_ref[h]
            tm = tmax_ref[h] + bias
            m_first = jnp.max(jnp.where(in_first, tm, NEG_INF), axis=0, keepdims=True)
            m_second = jnp.max(jnp.where(in_second, tm, NEG_INF), axis=0, keepdims=True)
            meff_ref[h] = jnp.where(in_first, m_first, m_second) - bias

    def prob_tile(t, h, s_ref, meff_ref, p_ref):
        p_ref[h, t] = jnp.exp2(s_ref[h, t] - meff_ref[h, t:t + 1, :]).astype(BF16)

    def value_tile(i, t, h, p_ref):
        if t in (0, nb):
            sel = 0 if t == 0 else 1
            k0 = q_offsets(i)[sel]
        else:
            sel, _, _, k0 = past_tile(i, t)
        acc_ref[sel, h] += _dot(vt_ref[h, :, pl.ds(k0, blk)], p_ref[h, t])

    def store(i):
        q_off = q_offsets(i)
        for sel in range(2):
            o_t = jnp.concatenate(
                [acc_ref[sel, h, 0:hd, :] * (1.0 / acc_ref[sel, h, hd:hd + 1, :]) for h in heads],
                axis=0)
            o_ref[0, pl.ds(q_off[sel], blk), :] = o_t.T.astype(BF16)

    tile_order = [0, nb] + list(range(1, nb))

    def pipeline_step(i, par, with_scores=True, with_values=True):
        cur, nxt = par, 1 - par
        chosen_max(i, tmax_refs[cur], bias_refs[cur], meff_refs[cur])
        if with_values:
            acc_ref[...] = jnp.zeros(acc_ref.shape, F32)
        for n, t in enumerate(tile_order):
            for h in heads:
                prob_tile(t, h, s_refs[cur], meff_refs[cur], p_refs[cur])
                if with_values:
                    value_tile(i - 1, t, h, p_refs[nxt])
                if with_scores:
                    score_tile(i + 1, t, h, s_refs[nxt], tmax_refs[nxt])
            if with_scores and n == len(tile_order) // 2:
                gates(i + 1, bias_refs[nxt])
        if with_values:
            store(i - 1)

    gates(0, bias_refs[0])
    for t in tile_order:
        for h in heads:
            score_tile(0, t, h, s_refs[0], tmax_refs[0])
    pipeline_step(0, 0, with_values=False)

    def two_pairs(d, carry):
        pipeline_step(2 * d + 1, 1)
        pipeline_step(2 * d + 2, 0)
        return carry

    lax.fori_loop(0, n_pairs // 2 - 1, two_pairs, 0)
    last = n_pairs - 1
    pipeline_step(last, last % 2, with_scores=False)
    acc_ref[...] = jnp.zeros(acc_ref.shape, F32)
    for t in tile_order:
        for h in heads:
            value_tile(last, t, h, p_refs[last % 2])
    store(last)


def _moba(q, k, v):
    B, S, aw = q.shape
    nb = S // MOBA_BLOCK
    blk, hps, hd = MOBA_BLOCK, HEADS_PER_SLAB, ATTN_HEAD_DIM
    assert nb % 4 == 0
    trows = -(-(nb + 1) // V7X_SUBLANES) * V7X_SUBLANES
    slab = pl.BlockSpec((1, S, V7X_LANES), lambda b, g: (b, 0, g))
    per_parity = [
        pltpu.VMEM((hps, nb + 1, blk, blk), F32),
        pltpu.VMEM((hps, nb + 1, blk, blk), BF16),
        pltpu.VMEM((hps, trows, blk), F32),
        pltpu.VMEM((hps, trows, blk), F32),
        pltpu.VMEM((hps, trows, blk), F32),
    ]
    scratch = [
        pltpu.VMEM((hps, V7X_LANES, S), BF16),
        pltpu.VMEM((hps, hd + V7X_BF16_ROWS, S), BF16),
        pltpu.VMEM((nb, V7X_LANES), F32),
        pltpu.VMEM((2, hps, hd + V7X_BF16_ROWS, blk), F32),
    ] + [spec for spec in per_parity for _ in range(2)]
    resident = (8 * S * V7X_LANES * 2 + (hps + 1) * S * V7X_LANES * 2
                + 2 * hps * (nb + 1) * blk * blk * 6 + 4 * blk * blk * 4)
    return pl.pallas_call(
        _moba_kernel,
        grid=(B, aw // V7X_LANES),
        in_specs=[slab, slab, slab],
        out_specs=slab,
        out_shape=jax.ShapeDtypeStruct((B, S, aw), BF16),
        scratch_shapes=scratch,
        compiler_params=pltpu.CompilerParams(
            dimension_semantics=("parallel", "parallel"),
            vmem_limit_bytes=_vmem_limit(resident)),
    )(q, k, v)


def _mem_fold_kernel(mem_ref, w_kv_ref, w_q_ref, w_o_ref, qk_ref, vo_ref, *, q_scale):
    M, D = mem_ref.shape[1], mem_ref.shape[2]
    xhd = D // XATTN_HEADS
    mb = mem_ref[0].astype(BF16)
    k = _dot(mb, w_kv_ref[:, 0:D]).astype(BF16)
    v = _dot(mb, w_kv_ref[:, D:2 * D]).astype(BF16)
    for h in range(XATTN_HEADS):
        cols = slice(h * xhd, (h + 1) * xhd)
        qk = lax.dot_general(w_q_ref[:, cols], k[:, cols], (((1,), (1,)), ((), ())),
                             preferred_element_type=F32)
        qk_ref[0, :, h * M:(h + 1) * M] = (qk * q_scale).astype(BF16)
        vo_ref[0, h * M:(h + 1) * M, :] = _dot(v[:, cols], w_o_ref[cols, :]).astype(BF16)


def _mem_fold(mem, w_xkv, w_xq, w_xo):
    B, M, D = mem.shape
    q_scale = (D // XATTN_HEADS) ** -0.5 * LOG2E
    hm = XATTN_HEADS * M
    resident = 2 * M * D * 4 + 4 * D * D * 2 + 4 * D * hm * 2 + 4 * M * D * 4
    return pl.pallas_call(
        functools.partial(_mem_fold_kernel, q_scale=q_scale),
        grid=(B,),
        in_specs=[pl.BlockSpec((1, M, D), lambda b: (b, 0, 0)), _const_spec(w_xkv.shape),
                  _const_spec(w_xq.shape), _const_spec(w_xo.shape)],
        out_specs=[pl.BlockSpec((1, D, hm), lambda b: (b, 0, 0)),
                   pl.BlockSpec((1, hm, D), lambda b: (b, 0, 0))],
        out_shape=[jax.ShapeDtypeStruct((B, D, hm), BF16), jax.ShapeDtypeStruct((B, hm, D), BF16)],
        compiler_params=pltpu.CompilerParams(
            dimension_semantics=("parallel",), vmem_limit_bytes=_vmem_limit(resident)),
    )(mem, w_xkv, w_xq, w_xo)


def _mix_xattn_kernel(x_ref, pool_ref, attn_ref, w_out_ref, g1_ref, b1_ref,
                      qk_ref, vo_ref, g2_ref, b2_ref, h_ref):
    D = x_ref.shape[2]
    pw = pool_ref.shape[2]
    M = qk_ref.shape[2] // XATTN_HEADS
    mix = _dot(pool_ref[0], w_out_ref[0:pw, :]) + _dot(attn_ref[0], w_out_ref[pw:D, :])
    h1 = _layer_norm(DEEPNORM_ALPHA * x_ref[0] + mix, g1_ref[...], b1_ref[...])

    s = _dot(h1.astype(BF16), qk_ref[0])
    probs = []
    for h in range(XATTN_HEADS):
        sh = s[:, h * M:(h + 1) * M]
        p = jnp.exp2(sh - jnp.max(sh, axis=-1, keepdims=True))
        probs.append((p * (1.0 / jnp.sum(p, axis=-1, keepdims=True))).astype(BF16))
    xo = _dot(jnp.concatenate(probs, axis=-1), vo_ref[0])
    h_ref[0] = _layer_norm(DEEPNORM_ALPHA * h1 + xo, g2_ref[...], b2_ref[...])


def _mix_xattn(x, pool, attn, w_out, g1, b1, qk, vo, g2, b2, *, ts):
    B, S, D = x.shape
    hm = qk.shape[2]
    tile = lambda width: pl.BlockSpec((1, ts, width), lambda b, s: (b, s, 0))
    vec = _const_spec((1, D))
    resident = (4 * ts * D * 4 + 4 * ts * pool.shape[2] * 2 + D * D * 2 + 4 * D * hm * 2
                + 6 * ts * D * 4)
    return pl.pallas_call(
        _mix_xattn_kernel,
        grid=(B, S // ts),
        in_specs=[tile(D), tile(pool.shape[2]), tile(attn.shape[2]), _const_spec(w_out.shape),
                  vec, vec, pl.BlockSpec((1, D, hm), lambda b, s: (b, 0, 0)),
                  pl.BlockSpec((1, hm, D), lambda b, s: (b, 0, 0)), vec, vec],
        out_specs=tile(D),
        out_shape=jax.ShapeDtypeStruct((B, S, D), F32),
        compiler_params=pltpu.CompilerParams(
            dimension_semantics=("parallel", "parallel"),
            vmem_limit_bytes=_vmem_limit(resident)),
    )(x, pool, attn, w_out, g1, b1, qk, vo, g2, b2)


def _ffn_kernel(h_ref, w_gate_ref, w_up_ref, w_down_ref, g_ref, b_ref, o_ref, *, ff_chunk):
    h = h_ref[...]
    hb = h.astype(BF16)
    d_ff = w_gate_ref.shape[1]
    y = jnp.zeros(h.shape, F32)
    for c in range(d_ff // ff_chunk):
        cols = slice(c * ff_chunk, (c + 1) * ff_chunk)
        gate = _dot(hb, w_gate_ref[:, cols])
        up = _dot(hb, w_up_ref[:, cols])
        act = gate * (1.0 / (1.0 + jnp.exp(-gate))) * up
        y = y + _dot(act.astype(BF16), w_down_ref[cols, :])
    o_ref[...] = _layer_norm(DEEPNORM_ALPHA * h + y, g_ref[...], b_ref[...])


def _ffn(h, w_gate, w_up, w_down, g, b, *, ts, ff_chunk):
    N, D = h.shape
    d_ff = w_gate.shape[1]
    tile = pl.BlockSpec((ts, D), lambda t: (t, 0))
    vec = _const_spec((1, D))
    resident = 4 * ts * D * 4 + 3 * D * d_ff * 2 + 3 * ts * ff_chunk * 4 + 2 * ts * D * 4
    return pl.pallas_call(
        functools.partial(_ffn_kernel, ff_chunk=ff_chunk),
        grid=(N // ts,),
        in_specs=[tile, _const_spec(w_gate.shape), _const_spec(w_up.shape),
                  _const_spec(w_down.shape), vec, vec],
        out_specs=tile,
        out_shape=jax.ShapeDtypeStruct((N, D), F32),
        compiler_params=pltpu.CompilerParams(
            dimension_semantics=("parallel",), vmem_limit_bytes=_vmem_limit(resident)),
    )(h, w_gate, w_up, w_down, g, b)


def kernel(x, mem, w_in, w_pool, pool_scale, w_out, ln1_g, ln1_b, w_xq, w_xkv, w_xo,
           ln2_g, ln2_b, w_gate, w_up, w_down, ln3_g, ln3_b):
    B, S, D = x.shape
    assert w_in.shape[0] == DEPTH and S % (4 * MOBA_BLOCK) == 0
    assert (w_in.shape[2] - POOL_WIDTH) % (3 * V7X_LANES) == 0
    h = x
    for l in range(DEPTH):
        wb = lambda w: w[l].astype(BF16)
        row = lambda p: p[l].reshape(1, -1)
        pool, q, k, v = _proj_pool(h, wb(w_in), wb(w_pool), row(pool_scale), ts=1024)
        attn = _moba(q, k, v)
        qk, vo = _mem_fold(mem, wb(w_xkv), wb(w_xq), wb(w_xo))
        h = _mix_xattn(h, pool, attn, wb(w_out), row(ln1_g), row(ln1_b), qk, vo,
                       row(ln2_g), row(ln2_b), ts=1024)
        h = _ffn(h.reshape(B * S, D), wb(w_gate), wb(w_up), wb(w_down),
                 row(ln3_g), row(ln3_b), ts=1024, ff_chunk=256).reshape(B, S, D)
    return h
```

```python
import functools
import math

import jax
import jax.numpy as jnp
from jax import lax
from jax.experimental import pallas as pl
from jax.experimental.pallas import tpu as pltpu

POOL_WINDOWS = (2, 4, 8, 16)
POOL_GROUP = 128
POOL_WIDTH = POOL_GROUP * len(POOL_WINDOWS)
POOL_HALO = max(POOL_WINDOWS)
ATTN_HEAD_DIM = 64
MOBA_BLOCK = 256
MOBA_TOPK = 3
XATTN_HEADS = 4
DEPTH = 1
DEEPNORM_ALPHA = (2.0 * DEPTH) ** 0.25
LN_EPS = 1e-5
LOG2E = math.log2(math.e)

V7X_LANES = 128
V7X_SUBLANES = 8
V7X_VMEM_BYTES = 64 * 1024 * 1024
V7X_BF16_ROWS = 16
HEADS_PER_SLAB = V7X_LANES // ATTN_HEAD_DIM

F32 = jnp.float32
BF16 = jnp.bfloat16
NEG_INF = float("-inf")


def _vmem_limit(resident_bytes):
    return int(min(2 * resident_bytes, V7X_VMEM_BYTES * 7 // 8))


def _const_spec(shape):
    return pl.BlockSpec(shape, lambda *_: (0,) * len(shape), pipeline_mode=pl.Buffered(1))


def _dot(a, b):
    return jnp.dot(a, b, preferred_element_type=F32)


def _layer_norm(y, g, b):
    mu = jnp.mean(y, axis=-1, keepdims=True)
    d = y - mu
    var = jnp.mean(d * d, axis=-1, keepdims=True)
    return d * lax.rsqrt(var + LN_EPS) * g + b


def _proj_pool_kernel(x_ref, w_in_ref, w_pool_ref, scale_ref,
                      pool_ref, q_ref, k_ref, v_ref, ubuf_ref, *, q_scale):
    s_idx = pl.program_id(1)
    ts = x_ref.shape[1]
    aw = q_ref.shape[2]
    xb = x_ref[0].astype(BF16)

    u = _dot(xb, w_in_ref[:, 0:POOL_WIDTH])
    q_ref[0] = (_dot(xb, w_in_ref[:, POOL_WIDTH:POOL_WIDTH + aw]) * q_scale).astype(BF16)
    k_ref[0] = _dot(xb, w_in_ref[:, POOL_WIDTH + aw:POOL_WIDTH + 2 * aw]).astype(BF16)
    v_ref[0] = _dot(xb, w_in_ref[:, POOL_WIDTH + 2 * aw:POOL_WIDTH + 3 * aw]).astype(BF16)

    @pl.when(s_idx == 0)
    def _():
        ubuf_ref[0:POOL_HALO, :] = jnp.zeros((POOL_HALO, POOL_WIDTH), F32)

    ubuf_ref[POOL_HALO:POOL_HALO + ts, :] = u
    pos = s_idx * ts + lax.broadcasted_iota(jnp.int32, (ts, POOL_GROUP), 0)
    for g, w in enumerate(POOL_WINDOWS):
        cols = slice(g * POOL_GROUP, (g + 1) * POOL_GROUP)
        win = ubuf_ref[:, cols]
        span = 1
        while span < w:
            win = win + pltpu.roll(win, span, axis=0)
            span *= 2
        inv_count = 1.0 / jnp.minimum(pos + 1, w).astype(F32)
        pooled = win[POOL_HALO:, :] * inv_count - u[:, cols]
        mixed = _dot(pooled.astype(BF16), w_pool_ref[g]) * scale_ref[:, cols]
        pool_ref[0, :, cols] = mixed.astype(BF16)
    ubuf_ref[0:POOL_HALO, :] = u[ts - POOL_HALO:, :]


def _proj_pool(x, w_in, w_pool, pool_scale, *, ts):
    B, S, D = x.shape
    aw = (w_in.shape[1] - POOL_WIDTH) // 3
    q_scale = ATTN_HEAD_DIM ** -0.5 * LOG2E
    slab = lambda width: pl.BlockSpec((1, ts, width), lambda b, s: (b, s, 0))
    resident = (2 * ts * D * 4 + D * w_in.shape[1] * 2 + 2 * ts * (POOL_WIDTH + 3 * aw) * 2
                + (ts + POOL_HALO) * POOL_WIDTH * 4 + 4 * ts * POOL_WIDTH * 4)
    return pl.pallas_call(
        functools.partial(_proj_pool_kernel, q_scale=q_scale),
        grid=(B, S // ts),
        in_specs=[slab(D), _const_spec(w_in.shape), _const_spec(w_pool.shape),
                  _const_spec(pool_scale.shape)],
        out_specs=[slab(POOL_WIDTH), slab(aw), slab(aw), slab(aw)],
        out_shape=[jax.ShapeDtypeStruct((B, S, POOL_WIDTH), BF16)]
        + [jax.ShapeDtypeStruct((B, S, aw), BF16)] * 3,
        scratch_shapes=[pltpu.VMEM((ts + POOL_HALO, POOL_WIDTH), F32)],
        compiler_params=pltpu.CompilerParams(
            dimension_semantics=("parallel", "arbitrary"),
            vmem_limit_bytes=_vmem_limit(resident)),
    )(x, w_in, w_pool, pool_scale)


def _moba_kernel(q_ref, k_ref, v_ref, o_ref, qt_ref, vt_ref, kmean_ref, acc_ref,
                 s0_ref, s1_ref, bias0_ref, bias1_ref,
                 meff0_ref, meff1_ref, tmax0_ref, tmax1_ref):
    S = q_ref.shape[1]
    nb = S // MOBA_BLOCK
    blk = MOBA_BLOCK
    hd = ATTN_HEAD_DIM
    heads = range(HEADS_PER_SLAB)
    n_pairs = nb // 2
    s_refs = (s0_ref, s1_ref)
    bias_refs, meff_refs = (bias0_ref, bias1_ref), (meff0_ref, meff1_ref)
    tmax_refs = (tmax0_ref, tmax1_ref)
    trows = tmax0_ref.shape[1]

    chan_head = lax.broadcasted_iota(jnp.int32, (V7X_LANES, blk), 0) // hd
    vrows = vt_ref.shape[1]
    ones_row = jnp.where(lax.broadcasted_iota(jnp.int32, (vrows - hd, blk), 0) == 0,
                         1.0, 0.0).astype(BF16)
    for c in range(nb):
        cols = slice(c * blk, (c + 1) * blk)
        qtf = q_ref[0, cols, :].astype(F32).T
        for h in heads:
            qt_ref[h, :, cols] = jnp.where(chan_head == h, qtf, 0.0).astype(BF16)
        vtb = v_ref[0, cols, :].astype(F32).T.astype(BF16)
        for h in heads:
            vt_ref[h, 0:hd, cols] = vtb[h * hd:(h + 1) * hd, :]
            vt_ref[h, hd:vrows, cols] = ones_row
        kmean_ref[c:c + 1, :] = jnp.mean(k_ref[0, cols, :].astype(F32), axis=0, keepdims=True)

    blk_idx = lax.broadcasted_iota(jnp.int32, (nb, blk), 0)
    tile_idx = lax.broadcasted_iota(jnp.int32, (trows, blk), 0)
    for tmax_ref in tmax_refs:
        tmax_ref[...] = jnp.full(tmax_ref.shape, NEG_INF, F32)
    causal = (lax.broadcasted_iota(jnp.int32, (blk, blk), 0)
              <= lax.broadcasted_iota(jnp.int32, (blk, blk), 1))

    def q_offsets(i):
        return [pl.multiple_of(qb * blk, blk) for qb in (i, nb - 1 - i)]

    def past_tile(i, t):
        first = t <= i
        sel = jnp.where(first, 0, 1)
        j = jnp.where(first, t - 1, t - i - 1)
        q_off = q_offsets(i)
        q0 = pl.multiple_of(jnp.where(first, q_off[0], q_off[1]), blk)
        return sel, q0, j, pl.multiple_of(j * blk, blk)

    def gates(i, bias_ref):
        never = jnp.full((trows - nb, blk), NEG_INF, F32)
        for h in heads:
            by_block = []
            for sel, qb in ((0, i), (1, nb - 1 - i)):
                q0 = pl.multiple_of(qb * blk, blk)
                qhf = qt_ref[h, :, pl.ds(q0, blk)].astype(F32)
                gate = jnp.dot(kmean_ref[...], qhf, preferred_element_type=F32,
                               precision=lax.Precision.HIGHEST)
                past = blk_idx < qb
                gate = jnp.where(past, gate, NEG_INF)
                rank = jnp.zeros((nb, blk), jnp.int32)
                for jp in range(nb):
                    other = gate[jp:jp + 1, :]
                    ahead = (other > gate) | ((other == gate) & (jp < blk_idx))
                    rank = rank + ahead.astype(jnp.int32)
                chosen = (rank < MOBA_TOPK) & past
                by_block.append(jnp.concatenate([jnp.where(chosen, 0.0, NEG_INF), never], axis=0))
            first = pltpu.roll(by_block[0], 1, axis=0)
            second = pltpu.roll(by_block[1], i + 1, axis=0)
            own = (tile_idx == 0) | (tile_idx == nb)
            bias_ref[h] = jnp.where(own, 0.0, jnp.where(tile_idx <= i, first, second))

    def score_tile(i, t, h, s_ref, tmax_ref):
        q_off = q_offsets(i)
        if t in (0, nb):
            q0 = k0 = q_off[0 if t == 0 else 1]
        else:
            _, q0, _, k0 = past_tile(i, t)
        s = _dot(k_ref[0, pl.ds(k0, blk), :], qt_ref[h, :, pl.ds(q0, blk)])
        if t in (0, nb):
            s = jnp.where(causal, s, NEG_INF)
        s_ref[h, t] = s
        tmax_ref[h, t:t + 1, :] = jnp.max(s, axis=0, keepdims=True)

    def chosen_max(i, tmax_ref, bias_ref, meff_ref):
        in_first = tile_idx <= i
        in_second = jnp.logical_and(tile_idx > i, tile_idx <= nb)
        for h in heads:
            bias = bias_ref[h]
            tm = tmax_ref[h] + bias
            m_first = jnp.max(jnp.where(in_first, tm, NEG_INF), axis=0, keepdims=True)
            m_second = jnp.max(jnp.where(in_second, tm, NEG_INF), axis=0, keepdims=True)
            meff_ref[h] = jnp.where(in_first, m_first, m_second) - bias

    def value_tile(i, t, h, s_ref, meff_ref):
        if t in (0, nb):
            sel = 0 if t == 0 else 1
            k0 = q_offsets(i)[sel]
        else:
            sel, _, _, k0 = past_tile(i, t)
        p = jnp.exp2(s_ref[h, t] - meff_ref[h, t:t + 1, :]).astype(BF16)
        acc_ref[sel, h] += _dot(vt_ref[h, :, pl.ds(k0, blk)], p)

    def store(i):
        q_off = q_offsets(i)
        for sel in range(2):
            o_t = jnp.concatenate(
                [acc_ref[sel, h, 0:hd, :] * (1.0 / acc_ref[sel, h, hd:hd + 1, :]) for h in heads],
                axis=0)
            o_ref[0, pl.ds(q_off[sel], blk), :] = o_t.T.astype(BF16)

    tile_order = [0, nb] + list(range(1, nb))

    def pipeline_step(i, par, with_scores=True):
        cur, nxt = par, 1 - par
        chosen_max(i, tmax_refs[cur], bias_refs[cur], meff_refs[cur])
        acc_ref[...] = jnp.zeros(acc_ref.shape, F32)
        for n, t in enumerate(tile_order):
            for h in heads:
                value_tile(i, t, h, s_refs[cur], meff_refs[cur])
                if with_scores:
                    score_tile(i + 1, t, h, s_refs[nxt], tmax_refs[nxt])
            if with_scores and n == len(tile_order) // 2:
                gates(i + 1, bias_refs[nxt])
        store(i)

    gates(0, bias_refs[0])
    for t in tile_order:
        for h in heads:
            score_tile(0, t, h, s_refs[0], tmax_refs[0])

    def two_pairs(d, carry):
        pipeline_step(2 * d, 0)
        pipeline_step(2 * d + 1, 1)
        return carry

    lax.fori_loop(0, n_pairs // 2 - 1, two_pairs, 0)
    pipeline_step(n_pairs - 2, 0)
    pipeline_step(n_pairs - 1, 1, with_scores=False)


def _moba(q, k, v):
    B, S, aw = q.shape
    nb = S // MOBA_BLOCK
    blk, hps, hd = MOBA_BLOCK, HEADS_PER_SLAB, ATTN_HEAD_DIM
    assert nb % 4 == 0
    trows = -(-(nb + 1) // V7X_SUBLANES) * V7X_SUBLANES
    slab = pl.BlockSpec((1, S, V7X_LANES), lambda b, g: (b, 0, g))
    per_parity = [
        pltpu.VMEM((hps, nb + 1, blk, blk), F32),
        pltpu.VMEM((hps, trows, blk), F32),
        pltpu.VMEM((hps, trows, blk), F32),
        pltpu.VMEM((hps, trows, blk), F32),
    ]
    scratch = [
        pltpu.VMEM((hps, V7X_LANES, S), BF16),
        pltpu.VMEM((hps, hd + V7X_BF16_ROWS, S), BF16),
        pltpu.VMEM((nb, V7X_LANES), F32),
        pltpu.VMEM((2, hps, hd + V7X_BF16_ROWS, blk), F32),
    ] + [spec for spec in per_parity for _ in range(2)]
    resident = (8 * S * V7X_LANES * 2 + (hps + 1) * S * V7X_LANES * 2
                + 2 * hps * (nb + 1) * blk * blk * 4 + 8 * blk * blk * 4)
    return pl.pallas_call(
        _moba_kernel,
        grid=(B, aw // V7X_LANES),
        in_specs=[slab, slab, slab],
        out_specs=slab,
        out_shape=jax.ShapeDtypeStruct((B, S, aw), BF16),
        scratch_shapes=scratch,
        compiler_params=pltpu.CompilerParams(
            dimension_semantics=("parallel", "parallel"),
            vmem_limit_bytes=_vmem_limit(resident)),
    )(q, k, v)


def _mem_fold_kernel(mem_ref, w_kv_ref, w_q_ref, w_o_ref, qk_ref, vo_ref, *, q_scale):
    M, D = mem_ref.shape[1], mem_ref.shape[2]
    xhd = D // XATTN_HEADS
    mb = mem_ref[0].astype(BF16)
    k = _dot(mb, w_kv_ref[:, 0:D]).astype(BF16)
    v = _dot(mb, w_kv_ref[:, D:2 * D]).astype(BF16)
    for h in range(XATTN_HEADS):
        cols = slice(h * xhd, (h + 1) * xhd)
        qk = lax.dot_general(w_q_ref[:, cols], k[:, cols], (((1,), (1,)), ((), ())),
                             preferred_element_type=F32)
        qk_ref[0, :, h * M:(h + 1) * M] = (qk * q_scale).astype(BF16)
        vo_ref[0, h * M:(h + 1) * M, :] = _dot(v[:, cols], w_o_ref[cols, :]).astype(BF16)


def _mem_fold(mem, w_xkv, w_xq, w_xo):
    B, M, D = mem.shape
    q_scale = (D // XATTN_HEADS) ** -0.5 * LOG2E
    hm = XATTN_HEADS * M
    resident = 2 * M * D * 4 + 4 * D * D * 2 + 4 * D * hm * 2 + 4 * M * D * 4
    return pl.pallas_call(
        functools.partial(_mem_fold_kernel, q_scale=q_scale),
        grid=(B,),
        in_specs=[pl.BlockSpec((1, M, D), lambda b: (b, 0, 0)), _const_spec(w_xkv.shape),
                  _const_spec(w_xq.shape), _const_spec(w_xo.shape)],
        out_specs=[pl.BlockSpec((1, D, hm), lambda b: (b, 0, 0)),
                   pl.BlockSpec((1, hm, D), lambda b: (b, 0, 0))],
        out_shape=[jax.ShapeDtypeStruct((B, D, hm), BF16), jax.ShapeDtypeStruct((B, hm, D), BF16)],
        compiler_params=pltpu.CompilerParams(
            dimension_semantics=("parallel",), vmem_limit_bytes=_vmem_limit(resident)),
    )(mem, w_xkv, w_xq, w_xo)


def _mix_xattn_kernel(x_ref, pool_ref, attn_ref, w_out_ref, g1_ref, b1_ref,
                      qk_ref, vo_ref, g2_ref, b2_ref, h_ref):
    D = x_ref.shape[2]
    pw = pool_ref.shape[2]
    M = qk_ref.shape[2] // XATTN_HEADS
    mix = _dot(pool_ref[0], w_out_ref[0:pw, :]) + _dot(attn_ref[0], w_out_ref[pw:D, :])
    h1 = _layer_norm(DEEPNORM_ALPHA * x_ref[0] + mix, g1_ref[...], b1_ref[...])

    s = _dot(h1.astype(BF16), qk_ref[0])
    probs = []
    for h in range(XATTN_HEADS):
        sh = s[:, h * M:(h + 1) * M]
        p = jnp.exp2(sh - jnp.max(sh, axis=-1, keepdims=True))
        probs.append((p * (1.0 / jnp.sum(p, axis=-1, keepdims=True))).astype(BF16))
    xo = _dot(jnp.concatenate(probs, axis=-1), vo_ref[0])
    h_ref[0] = _layer_norm(DEEPNORM_ALPHA * h1 + xo, g2_ref[...], b2_ref[...])


def _mix_xattn(x, pool, attn, w_out, g1, b1, qk, vo, g2, b2, *, ts):
    B, S, D = x.shape
    hm = qk.shape[2]
    tile = lambda width: pl.BlockSpec((1, ts, width), lambda b, s: (b, s, 0))
    vec = _const_spec((1, D))
    resident = (4 * ts * D * 4 + 4 * ts * pool.shape[2] * 2 + D * D * 2 + 4 * D * hm * 2
                + 6 * ts * D * 4)
    return pl.pallas_call(
        _mix_xattn_kernel,
        grid=(B, S // ts),
        in_specs=[tile(D), tile(pool.shape[2]), tile(attn.shape[2]), _const_spec(w_out.shape),
                  vec, vec, pl.BlockSpec((1, D, hm), lambda b, s: (b, 0, 0)),
                  pl.BlockSpec((1, hm, D), lambda b, s: (b, 0, 0)), vec, vec],
        out_specs=tile(D),
        out_shape=jax.ShapeDtypeStruct((B, S, D), F32),
        compiler_params=pltpu.CompilerParams(
            dimension_semantics=("parallel", "parallel"),
            vmem_limit_bytes=_vmem_limit(resident)),
    )(x, pool, attn, w_out, g1, b1, qk, vo, g2, b2)


def _ffn_kernel(h_ref, w_gate_ref, w_up_ref, w_down_ref, g_ref, b_ref, o_ref, *, ff_chunk):
    h = h_ref[...]
    hb = h.astype(BF16)
    d_ff = w_gate_ref.shape[1]
    y = jnp.zeros(h.shape, F32)
    for c in range(d_ff // ff_chunk):
        cols = slice(c * ff_chunk, (c + 1) * ff_chunk)
        gate = _dot(hb, w_gate_ref[:, cols])
        up = _dot(hb, w_up_ref[:, cols])
        act = gate * (1.0 / (1.0 + jnp.exp(-gate))) * up
        y = y + _dot(act.astype(BF16), w_down_ref[cols, :])
    o_ref[...] = _layer_norm(DEEPNORM_ALPHA * h + y, g_ref[...], b_ref[...])


def _ffn(h, w_gate, w_up, w_down, g, b, *, ts, ff_chunk):
    N, D = h.shape
    d_ff = w_gate.shape[1]
    tile = pl.BlockSpec((ts, D), lambda t: (t, 0))
    vec = _const_spec((1, D))
    resident = 4 * ts * D * 4 + 3 * D * d_ff * 2 + 3 * ts * ff_chunk * 4 + 2 * ts * D * 4
    return pl.pallas_call(
        functools.partial(_ffn_kernel, ff_chunk=ff_chunk),
        grid=(N // ts,),
        in_specs=[tile, _const_spec(w_gate.shape), _const_spec(w_up.shape),
                  _const_spec(w_down.shape), vec, vec],
        out_specs=tile,
        out_shape=jax.ShapeDtypeStruct((N, D), F32),
        compiler_params=pltpu.CompilerParams(
            dimension_semantics=("parallel",), vmem_limit_bytes=_vmem_limit(resident)),
    )(h, w_gate, w_up, w_down, g, b)


def kernel(x, mem, w_in, w_pool, pool_scale, w_out, ln1_g, ln1_b, w_xq, w_xkv, w_xo,
           ln2_g, ln2_b, w_gate, w_up, w_down, ln3_g, ln3_b):
    B, S, D = x.shape
    assert w_in.shape[0] == DEPTH and S % (4 * MOBA_BLOCK) == 0
    assert (w_in.shape[2] - POOL_WIDTH) % (3 * V7X_LANES) == 0
    h = x
    for l in range(DEPTH):
        wb = lambda w: w[l].astype(BF16)
        row = lambda p: p[l].reshape(1, -1)
        pool, q, k, v = _proj_pool(h, wb(w_in), wb(w_pool), row(pool_scale), ts=1024)
        attn = _moba(q, k, v)
        qk, vo = _mem_fold(mem, wb(w_xkv), wb(w_xq), wb(w_xo))
        h = _mix_xattn(h, pool, attn, wb(w_out), row(ln1_g), row(ln1_b), qk, vo,
                       row(ln2_g), row(ln2_b), ts=1024)
        h = _ffn(h.reshape(B * S, D), wb(w_gate), wb(w_up), wb(w_down),
                 row(ln3_g), row(ln3_b), ts=1024, ff_chunk=256).reshape(B, S, D)
    return h
```

```python
import functools
import math

import jax
import jax.numpy as jnp
from jax import lax
from jax.experimental import pallas as pl
from jax.experimental.pallas import tpu as pltpu

POOL_WINDOWS = (2, 4, 8, 16)
POOL_GROUP = 128
POOL_WIDTH = POOL_GROUP * len(POOL_WINDOWS)
POOL_HALO = max(POOL_WINDOWS)
ATTN_HEAD_DIM = 64
MOBA_BLOCK = 256
MOBA_TOPK = 3
XATTN_HEADS = 4
DEPTH = 1
DEEPNORM_ALPHA = (2.0 * DEPTH) ** 0.25
LN_EPS = 1e-5
LOG2E = math.log2(math.e)

V7X_LANES = 128
V7X_SUBLANES = 8
V7X_VMEM_BYTES = 64 * 1024 * 1024
V7X_BF16_ROWS = 16
HEADS_PER_SLAB = V7X_LANES // ATTN_HEAD_DIM

F32 = jnp.float32
BF16 = jnp.bfloat16
NEG_INF = float("-inf")


def _vmem_limit(resident_bytes):
    return int(min(2 * resident_bytes, V7X_VMEM_BYTES * 7 // 8))


def _const_spec(shape):
    return pl.BlockSpec(shape, lambda *_: (0,) * len(shape), pipeline_mode=pl.Buffered(1))


def _dot(a, b):
    return jnp.dot(a, b, preferred_element_type=F32)


def _layer_norm(y, g, b):
    mu = jnp.mean(y, axis=-1, keepdims=True)
    d = y - mu
    var = jnp.mean(d * d, axis=-1, keepdims=True)
    return d * lax.rsqrt(var + LN_EPS) * g + b


def _proj_pool_kernel(x_ref, w_in_ref, w_pool_ref, scale_ref,
                      pool_ref, q_ref, k_ref, v_ref, ubuf_ref, *, q_scale):
    s_idx = pl.program_id(1)
    ts = x_ref.shape[1]
    aw = q_ref.shape[2]
    chunk = ubuf_ref.shape[0] - POOL_HALO

    @pl.when(s_idx == 0)
    def _():
        ubuf_ref[0:POOL_HALO, :] = jnp.zeros((POOL_HALO, POOL_WIDTH), F32)

    for c in range(ts // chunk):
        rows = slice(c * chunk, (c + 1) * chunk)
        xb = x_ref[0, rows, :].astype(BF16)
        u = _dot(xb, w_in_ref[:, 0:POOL_WIDTH])
        ubuf_ref[POOL_HALO:POOL_HALO + chunk, :] = u
        head_pos = (s_idx * ts + c * chunk
                    + lax.broadcasted_iota(jnp.int32, (POOL_HALO, POOL_GROUP), 0))
        for g, w in enumerate(POOL_WINDOWS):
            cols = slice(g * POOL_GROUP, (g + 1) * POOL_GROUP)
            win = ubuf_ref[:, cols]
            span = 1
            while span < w:
                win = win + pltpu.roll(win, span, axis=0)
                span *= 2
            win = win[POOL_HALO:, :]
            head_inv = 1.0 / jnp.minimum(head_pos + 1, w).astype(F32)
            mean = jnp.concatenate(
                [win[0:POOL_HALO, :] * head_inv, win[POOL_HALO:, :] * (1.0 / w)], axis=0)
            mixed = _dot((mean - u[:, cols]).astype(BF16), w_pool_ref[g]) * scale_ref[:, cols]
            pool_ref[0, rows, cols] = mixed.astype(BF16)
        ubuf_ref[0:POOL_HALO, :] = u[chunk - POOL_HALO:, :]
        q_ref[0, rows, :] = (_dot(xb, w_in_ref[:, POOL_WIDTH:POOL_WIDTH + aw]) * q_scale).astype(BF16)
        k_ref[0, rows, :] = _dot(xb, w_in_ref[:, POOL_WIDTH + aw:POOL_WIDTH + 2 * aw]).astype(BF16)
        v_ref[0, rows, :] = _dot(xb, w_in_ref[:, POOL_WIDTH + 2 * aw:POOL_WIDTH + 3 * aw]).astype(BF16)


def _proj_pool(x, w_in, w_pool, pool_scale, *, ts, chunk):
    B, S, D = x.shape
    aw = (w_in.shape[1] - POOL_WIDTH) // 3
    q_scale = ATTN_HEAD_DIM ** -0.5 * LOG2E
    slab = lambda width: pl.BlockSpec((1, ts, width), lambda b, s: (b, s, 0))
    assert ts % chunk == 0 and chunk > POOL_HALO
    resident = (2 * ts * D * 4 + D * w_in.shape[1] * 2 + 2 * ts * (POOL_WIDTH + 3 * aw) * 2
                + (chunk + POOL_HALO) * POOL_WIDTH * 4 + 4 * chunk * w_in.shape[1] * 4)
    return pl.pallas_call(
        functools.partial(_proj_pool_kernel, q_scale=q_scale),
        grid=(B, S // ts),
        in_specs=[slab(D), _const_spec(w_in.shape), _const_spec(w_pool.shape),
                  _const_spec(pool_scale.shape)],
        out_specs=[slab(POOL_WIDTH), slab(aw), slab(aw), slab(aw)],
        out_shape=[jax.ShapeDtypeStruct((B, S, POOL_WIDTH), BF16)]
        + [jax.ShapeDtypeStruct((B, S, aw), BF16)] * 3,
        scratch_shapes=[pltpu.VMEM((chunk + POOL_HALO, POOL_WIDTH), F32)],
        compiler_params=pltpu.CompilerParams(
            dimension_semantics=("parallel", "arbitrary"),
            vmem_limit_bytes=_vmem_limit(resident)),
    )(x, w_in, w_pool, pool_scale)


def _moba_kernel(q_ref, k_ref, v_ref, o_ref, qt_ref, vt_ref, kmean_ref, acc_ref,
                 s0_ref, s1_ref, bias0_ref, bias1_ref,
                 meff0_ref, meff1_ref, tmax0_ref, tmax1_ref):
    S = q_ref.shape[1]
    nb = S // MOBA_BLOCK
    blk = MOBA_BLOCK
    hd = ATTN_HEAD_DIM
    heads = range(HEADS_PER_SLAB)
    n_pairs = nb // 2
    s_refs = (s0_ref, s1_ref)
    bias_refs, meff_refs = (bias0_ref, bias1_ref), (meff0_ref, meff1_ref)
    tmax_refs = (tmax0_ref, tmax1_ref)
    trows = tmax0_ref.shape[1]

    chan_head = lax.broadcasted_iota(jnp.int32, (V7X_LANES, blk), 0) // hd
    vrows = vt_ref.shape[1]
    ones_row = jnp.where(lax.broadcasted_iota(jnp.int32, (vrows - hd, blk), 0) == 0,
                         1.0, 0.0).astype(BF16)
    for c in range(nb):
        cols = slice(c * blk, (c + 1) * blk)
        qtf = q_ref[0, cols, :].astype(F32).T
        for h in heads:
            qt_ref[h, :, cols] = jnp.where(chan_head == h, qtf, 0.0).astype(BF16)
        vtb = v_ref[0, cols, :].astype(F32).T.astype(BF16)
        for h in heads:
            vt_ref[h, 0:hd, cols] = vtb[h * hd:(h + 1) * hd, :]
            vt_ref[h, hd:vrows, cols] = ones_row
        kmean_ref[c:c + 1, :] = jnp.mean(k_ref[0, cols, :].astype(F32), axis=0, keepdims=True)

    blk_idx = lax.broadcasted_iota(jnp.int32, (nb, blk), 0)
    tile_idx = lax.broadcasted_iota(jnp.int32, (trows, blk), 0)
    for tmax_ref in tmax_refs:
        tmax_ref[...] = jnp.full(tmax_ref.shape, NEG_INF, F32)
    causal = (lax.broadcasted_iota(jnp.int32, (blk, blk), 0)
              <= lax.broadcasted_iota(jnp.int32, (blk, blk), 1))

    def q_offsets(i):
        return [pl.multiple_of(qb * blk, blk) for qb in (i, nb - 1 - i)]

    def past_tile(i, t):
        first = t <= i
        sel = jnp.where(first, 0, 1)
        j = jnp.where(first, t - 1, t - i - 1)
        q_off = q_offsets(i)
        q0 = pl.multiple_of(jnp.where(first, q_off[0], q_off[1]), blk)
        return sel, q0, j, pl.multiple_of(j * blk, blk)

    def gates(i, bias_ref):
        never = jnp.full((trows - nb, blk), NEG_INF, F32)
        for h in heads:
            by_block = []
            for sel, qb in ((0, i), (1, nb - 1 - i)):
                q0 = pl.multiple_of(qb * blk, blk)
                qhf = qt_ref[h, :, pl.ds(q0, blk)].astype(F32)
                gate = jnp.dot(kmean_ref[...], qhf, preferred_element_type=F32,
                               precision=lax.Precision.HIGHEST)
                past = blk_idx < qb
                gate = jnp.where(past, gate, NEG_INF)
                bias = jnp.full((nb, blk), NEG_INF, F32)
                for _ in range(MOBA_TOPK):
                    top = jnp.max(gate, axis=0, keepdims=True)
                    lowest = jnp.min(jnp.where(gate == top, blk_idx, nb), axis=0, keepdims=True)
                    taken = blk_idx == lowest
                    bias = jnp.where(taken, 0.0, bias)
                    gate = jnp.where(taken, NEG_INF, gate)
                bias = jnp.where(past, bias, NEG_INF)
                by_block.append(jnp.concatenate([bias, never], axis=0))
            first = pltpu.roll(by_block[0], 1, axis=0)
            second = pltpu.roll(by_block[1], i + 1, axis=0)
            own = (tile_idx == 0) | (tile_idx == nb)
            bias_ref[h] = jnp.where(own, 0.0, jnp.where(tile_idx <= i, first, second))

    def score_tile(i, t, h, s_ref, tmax_ref):
        q_off = q_offsets(i)
        if t in (0, nb):
            q0 = k0 = q_off[0 if t == 0 else 1]
        else:
            _, q0, _, k0 = past_tile(i, t)
        s = _dot(k_ref[0, pl.ds(k0, blk), :], qt_ref[h, :, pl.ds(q0, blk)])
        if t in (0, nb):
            s = jnp.where(causal, s, NEG_INF)
        s_ref[h, t] = s
        tmax_ref[h, t:t + 1, :] = jnp.max(s, axis=0, keepdims=True)

    def chosen_max(i, tmax_ref, bias_ref, meff_ref):
        in_first = tile_idx <= i
        in_second = jnp.logical_and(tile_idx > i, tile_idx <= nb)
        for h in heads:
            bias = bias_ref[h]
            tm = tmax_ref[h] + bias
            m_first = jnp.max(jnp.where(in_first, tm, NEG_INF), axis=0, keepdims=True)
            m_second = jnp.max(jnp.where(in_second, tm, NEG_INF), axis=0, keepdims=True)
            meff_ref[h] = jnp.where(in_first, m_first, m_second) - bias

    def value_tile(i, t, h, s_ref, meff_ref):
        if t in (0, nb):
            sel = 0 if t == 0 else 1
            k0 = q_offsets(i)[sel]
        else:
            sel, _, _, k0 = past_tile(i, t)
        p = jnp.exp2(s_ref[h, t] - meff_ref[h, t:t + 1, :]).astype(BF16)
        acc_ref[sel, h] += _dot(vt_ref[h, :, pl.ds(k0, blk)], p)

    def store(i):
        q_off = q_offsets(i)
        for sel in range(2):
            o_t = jnp.concatenate(
                [acc_ref[sel, h, 0:hd, :] * (1.0 / acc_ref[sel, h, hd:hd + 1, :]) for h in heads],
                axis=0)
            o_ref[0, pl.ds(q_off[sel], blk), :] = o_t.T.astype(BF16)

    tile_order = [0, nb] + list(range(1, nb))

    def pipeline_step(i, par, with_scores=True):
        cur, nxt = par, 1 - par
        chosen_max(i, tmax_refs[cur], bias_refs[cur], meff_refs[cur])
        acc_ref[...] = jnp.zeros(acc_ref.shape, F32)
        for n, t in enumerate(tile_order):
            for h in heads:
                value_tile(i, t, h, s_refs[cur], meff_refs[cur])
                if with_scores:
                    score_tile(i + 1, t, h, s_refs[nxt], tmax_refs[nxt])
            if with_scores and n == len(tile_order) // 2:
                gates(i + 1, bias_refs[nxt])
        store(i)

    gates(0, bias_refs[0])
    for t in tile_order:
        for h in heads:
            score_tile(0, t, h, s_refs[0], tmax_refs[0])

    def two_pairs(d, carry):
        pipeline_step(2 * d, 0)
        pipeline_step(2 * d + 1, 1)
        return carry

    lax.fori_loop(0, n_pairs // 2 - 1, two_pairs, 0)
    pipeline_step(n_pairs - 2, 0)
    pipeline_step(n_pairs - 1, 1, with_scores=False)


def _moba(q, k, v):
    B, S, aw = q.shape
    nb = S // MOBA_BLOCK
    blk, hps, hd = MOBA_BLOCK, HEADS_PER_SLAB, ATTN_HEAD_DIM
    assert nb % 4 == 0
    trows = -(-(nb + 1) // V7X_SUBLANES) * V7X_SUBLANES
    slab = pl.BlockSpec((1, S, V7X_LANES), lambda b, g: (b, 0, g))
    per_parity = [
        pltpu.VMEM((hps, nb + 1, blk, blk), F32),
        pltpu.VMEM((hps, trows, blk), F32),
        pltpu.VMEM((hps, trows, blk), F32),
        pltpu.VMEM((hps, trows, blk), F32),
    ]
    scratch = [
        pltpu.VMEM((hps, V7X_LANES, S), BF16),
        pltpu.VMEM((hps, hd + V7X_BF16_ROWS, S), BF16),
        pltpu.VMEM((nb, V7X_LANES), F32),
        pltpu.VMEM((2, hps, hd + V7X_BF16_ROWS, blk), F32),
    ] + [spec for spec in per_parity for _ in range(2)]
    resident = (8 * S * V7X_LANES * 2 + (hps + 1) * S * V7X_LANES * 2
                + 2 * hps * (nb + 1) * blk * blk * 4 + 8 * blk * blk * 4)
    return pl.pallas_call(
        _moba_kernel,
        grid=(B, aw // V7X_LANES),
        in_specs=[slab, slab, slab],
        out_specs=slab,
        out_shape=jax.ShapeDtypeStruct((B, S, aw), BF16),
        scratch_shapes=scratch,
        compiler_params=pltpu.CompilerParams(
            dimension_semantics=("parallel", "parallel"),
            vmem_limit_bytes=_vmem_limit(resident)),
    )(q, k, v)


def _mem_fold_kernel(mem_ref, w_kv_ref, w_q_ref, w_o_ref, qk_ref, vo_ref, *, q_scale):
    M, D = mem_ref.shape[1], mem_ref.shape[2]
    xhd = D // XATTN_HEADS
    mb = mem_ref[0].astype(BF16)
    k = _dot(mb, w_kv_ref[:, 0:D]).astype(BF16)
    v = _dot(mb, w_kv_ref[:, D:2 * D]).astype(BF16)
    for h in range(XATTN_HEADS):
        cols = slice(h * xhd, (h + 1) * xhd)
        qk = lax.dot_general(w_q_ref[:, cols], k[:, cols], (((1,), (1,)), ((), ())),
                             preferred_element_type=F32)
        qk_ref[0, :, h * M:(h + 1) * M] = (qk * q_scale).astype(BF16)
        vo_ref[0, h * M:(h + 1) * M, :] = _dot(v[:, cols], w_o_ref[cols, :]).astype(BF16)


def _mem_fold(mem, w_xkv, w_xq, w_xo):
    B, M, D = mem.shape
    q_scale = (D // XATTN_HEADS) ** -0.5 * LOG2E
    hm = XATTN_HEADS * M
    resident = 2 * M * D * 4 + 4 * D * D * 2 + 4 * D * hm * 2 + 4 * M * D * 4
    return pl.pallas_call(
        functools.partial(_mem_fold_kernel, q_scale=q_scale),
        grid=(B,),
        in_specs=[pl.BlockSpec((1, M, D), lambda b: (b, 0, 0)), _const_spec(w_xkv.shape),
                  _const_spec(w_xq.shape), _const_spec(w_xo.shape)],
        out_specs=[pl.BlockSpec((1, D, hm), lambda b: (b, 0, 0)),
                   pl.BlockSpec((1, hm, D), lambda b: (b, 0, 0))],
        out_shape=[jax.ShapeDtypeStruct((B, D, hm), BF16), jax.ShapeDtypeStruct((B, hm, D), BF16)],
        compiler_params=pltpu.CompilerParams(
            dimension_semantics=("parallel",), vmem_limit_bytes=_vmem_limit(resident)),
    )(mem, w_xkv, w_xq, w_xo)


def _mix_xattn_kernel(x_ref, pool_ref, attn_ref, w_out_ref, g1_ref, b1_ref,
                      qk_ref, vo_ref, g2_ref, b2_ref, h_ref):
    D = x_ref.shape[2]
    pw = pool_ref.shape[2]
    M = qk_ref.shape[2] // XATTN_HEADS
    mix = _dot(pool_ref[0], w_out_ref[0:pw, :]) + _dot(attn_ref[0], w_out_ref[pw:D, :])
    h1 = _layer_norm(DEEPNORM_ALPHA * x_ref[0] + mix, g1_ref[...], b1_ref[...])

    s = _dot(h1.astype(BF16), qk_ref[0])
    probs = []
    for h in range(XATTN_HEADS):
        sh = s[:, h * M:(h + 1) * M]
        p = jnp.exp2(sh - jnp.max(sh, axis=-1, keepdims=True))
        probs.append((p * (1.0 / jnp.sum(p, axis=-1, keepdims=True))).astype(BF16))
    xo = _dot(jnp.concatenate(probs, axis=-1), vo_ref[0])
    h_ref[0] = _layer_norm(DEEPNORM_ALPHA * h1 + xo, g2_ref[...], b2_ref[...])


def _mix_xattn(x, pool, attn, w_out, g1, b1, qk, vo, g2, b2, *, ts):
    B, S, D = x.shape
    hm = qk.shape[2]
    tile = lambda width: pl.BlockSpec((1, ts, width), lambda b, s: (b, s, 0))
    vec = _const_spec((1, D))
    resident = (4 * ts * D * 4 + 4 * ts * pool.shape[2] * 2 + D * D * 2 + 4 * D * hm * 2
                + 6 * ts * D * 4)
    return pl.pallas_call(
        _mix_xattn_kernel,
        grid=(B, S // ts),
        in_specs=[tile(D), tile(pool.shape[2]), tile(attn.shape[2]), _const_spec(w_out.shape),
                  vec, vec, pl.BlockSpec((1, D, hm), lambda b, s: (b, 0, 0)),
                  pl.BlockSpec((1, hm, D), lambda b, s: (b, 0, 0)), vec, vec],
        out_specs=tile(D),
        out_shape=jax.ShapeDtypeStruct((B, S, D), F32),
        compiler_params=pltpu.CompilerParams(
            dimension_semantics=("parallel", "parallel"),
            vmem_limit_bytes=_vmem_limit(resident)),
    )(x, pool, attn, w_out, g1, b1, qk, vo, g2, b2)


def _ffn_kernel(h_ref, w_gate_ref, w_up_ref, w_down_ref, g_ref, b_ref, o_ref, *, ff_chunk):
    h = h_ref[...]
    hb = h.astype(BF16)
    d_ff = w_gate_ref.shape[1]
    y = jnp.zeros(h.shape, F32)
    for c in range(d_ff // ff_chunk):
        cols = slice(c * ff_chunk, (c + 1) * ff_chunk)
        gate = _dot(hb, w_gate_ref[:, cols])
        up = _dot(hb, w_up_ref[:, cols])
        act = gate * (1.0 / (1.0 + jnp.exp(-gate))) * up
        y = y + _dot(act.astype(BF16), w_down_ref[cols, :])
    o_ref[...] = _layer_norm(DEEPNORM_ALPHA * h + y, g_ref[...], b_ref[...])


def _ffn(h, w_gate, w_up, w_down, g, b, *, ts, ff_chunk):
    N, D = h.shape
    d_ff = w_gate.shape[1]
    tile = pl.BlockSpec((ts, D), lambda t: (t, 0))
    vec = _const_spec((1, D))
    resident = 4 * ts * D * 4 + 3 * D * d_ff * 2 + 3 * ts * ff_chunk * 4 + 2 * ts * D * 4
    return pl.pallas_call(
        functools.partial(_ffn_kernel, ff_chunk=ff_chunk),
        grid=(N // ts,),
        in_specs=[tile, _const_spec(w_gate.shape), _const_spec(w_up.shape),
                  _const_spec(w_down.shape), vec, vec],
        out_specs=tile,
        out_shape=jax.ShapeDtypeStruct((N, D), F32),
        compiler_params=pltpu.CompilerParams(
            dimension_semantics=("parallel",), vmem_limit_bytes=_vmem_limit(resident)),
    )(h, w_gate, w_up, w_down, g, b)


def kernel(x, mem, w_in, w_pool, pool_scale, w_out, ln1_g, ln1_b, w_xq, w_xkv, w_xo,
           ln2_g, ln2_b, w_gate, w_up, w_down, ln3_g, ln3_b):
    B, S, D = x.shape
    assert w_in.shape[0] == DEPTH and S % (4 * MOBA_BLOCK) == 0
    assert (w_in.shape[2] - POOL_WIDTH) % (3 * V7X_LANES) == 0
    h = x
    for l in range(DEPTH):
        wb = lambda w: w[l].astype(BF16)
        row = lambda p: p[l].reshape(1, -1)
        pool, q, k, v = _proj_pool(h, wb(w_in), wb(w_pool), row(pool_scale), ts=1024,
                                   chunk=512)
        attn = _moba(q, k, v)
        qk, vo = _mem_fold(mem, wb(w_xkv), wb(w_xq), wb(w_xo))
        h = _mix_xattn(h, pool, attn, wb(w_out), row(ln1_g), row(ln1_b), qk, vo,
                       row(ln2_g), row(ln2_b), ts=1024)
        h = _ffn(h.reshape(B * S, D), wb(w_gate), wb(w_up), wb(w_down),
                 row(ln3_g), row(ln3_b), ts=1024, ff_chunk=256).reshape(B, S, D)
    return h
```

```python
import functools
import math

import jax
import jax.numpy as jnp
from jax import lax
from jax.experimental import pallas as pl
from jax.experimental.pallas import tpu as pltpu

POOL_WINDOWS = (2, 4, 8, 16)
POOL_GROUP = 128
POOL_WIDTH = POOL_GROUP * len(POOL_WINDOWS)
POOL_HALO = max(POOL_WINDOWS)
ATTN_HEAD_DIM = 64
MOBA_BLOCK = 256
MOBA_TOPK = 3
XATTN_HEADS = 4
DEPTH = 1
DEEPNORM_ALPHA = (2.0 * DEPTH) ** 0.25
LN_EPS = 1e-5
LOG2E = math.log2(math.e)

V7X_LANES = 128
V7X_SUBLANES = 8
V7X_VMEM_BYTES = 64 * 1024 * 1024
V7X_BF16_ROWS = 16
F32_AS_BF16_PARTS = 3
HEADS_PER_SLAB = V7X_LANES // ATTN_HEAD_DIM

F32 = jnp.float32
BF16 = jnp.bfloat16
NEG_INF = float("-inf")


def _vmem_limit(resident_bytes):
    return int(min(2 * resident_bytes, V7X_VMEM_BYTES * 7 // 8))


def _const_spec(shape):
    return pl.BlockSpec(shape, lambda *_: (0,) * len(shape), pipeline_mode=pl.Buffered(1))


def _dot(a, b):
    return jnp.dot(a, b, preferred_element_type=F32)


def _layer_norm(y, g, b):
    mu = jnp.mean(y, axis=-1, keepdims=True)
    d = y - mu
    var = jnp.mean(d * d, axis=-1, keepdims=True)
    return d * lax.rsqrt(var + LN_EPS) * g + b


def _proj_pool_kernel(x_ref, w_in_ref, w_pool_ref, scale_ref,
                      pool_ref, q_ref, k_ref, v_ref, ubuf_ref, *, q_scale):
    s_idx = pl.program_id(1)
    ts = x_ref.shape[1]
    aw = q_ref.shape[2]
    chunk = ubuf_ref.shape[0] - POOL_HALO

    @pl.when(s_idx == 0)
    def _():
        ubuf_ref[0:POOL_HALO, :] = jnp.zeros((POOL_HALO, POOL_WIDTH), F32)

    for c in range(ts // chunk):
        rows = slice(c * chunk, (c + 1) * chunk)
        xb = x_ref[0, rows, :].astype(BF16)
        u = _dot(xb, w_in_ref[:, 0:POOL_WIDTH])
        ubuf_ref[POOL_HALO:POOL_HALO + chunk, :] = u
        head_pos = (s_idx * ts + c * chunk
                    + lax.broadcasted_iota(jnp.int32, (POOL_HALO, POOL_GROUP), 0))
        for g, w in enumerate(POOL_WINDOWS):
            cols = slice(g * POOL_GROUP, (g + 1) * POOL_GROUP)
            win = ubuf_ref[:, cols]
            span = 1
            while span < w:
                win = win + pltpu.roll(win, span, axis=0)
                span *= 2
            win = win[POOL_HALO:, :]
            head_inv = 1.0 / jnp.minimum(head_pos + 1, w).astype(F32)
            mean = jnp.concatenate(
                [win[0:POOL_HALO, :] * head_inv, win[POOL_HALO:, :] * (1.0 / w)], axis=0)
            mixed = _dot((mean - u[:, cols]).astype(BF16), w_pool_ref[g]) * scale_ref[:, cols]
            pool_ref[0, rows, cols] = mixed.astype(BF16)
        ubuf_ref[0:POOL_HALO, :] = u[chunk - POOL_HALO:, :]
        q_ref[0, rows, :] = (_dot(xb, w_in_ref[:, POOL_WIDTH:POOL_WIDTH + aw]) * q_scale).astype(BF16)
        k_ref[0, rows, :] = _dot(xb, w_in_ref[:, POOL_WIDTH + aw:POOL_WIDTH + 2 * aw]).astype(BF16)
        v_ref[0, rows, :] = _dot(xb, w_in_ref[:, POOL_WIDTH + 2 * aw:POOL_WIDTH + 3 * aw]).astype(BF16)


def _proj_pool(x, w_in, w_pool, pool_scale, *, ts, chunk):
    B, S, D = x.shape
    aw = (w_in.shape[1] - POOL_WIDTH) // 3
    q_scale = ATTN_HEAD_DIM ** -0.5 * LOG2E
    slab = lambda width: pl.BlockSpec((1, ts, width), lambda b, s: (b, s, 0))
    assert ts % chunk == 0 and chunk > POOL_HALO
    resident = (2 * ts * D * 4 + D * w_in.shape[1] * 2 + 2 * ts * (POOL_WIDTH + 3 * aw) * 2
                + (chunk + POOL_HALO) * POOL_WIDTH * 4 + 4 * chunk * w_in.shape[1] * 4)
    return pl.pallas_call(
        functools.partial(_proj_pool_kernel, q_scale=q_scale),
        grid=(B, S // ts),
        in_specs=[slab(D), _const_spec(w_in.shape), _const_spec(w_pool.shape),
                  _const_spec(pool_scale.shape)],
        out_specs=[slab(POOL_WIDTH), slab(aw), slab(aw), slab(aw)],
        out_shape=[jax.ShapeDtypeStruct((B, S, POOL_WIDTH), BF16)]
        + [jax.ShapeDtypeStruct((B, S, aw), BF16)] * 3,
        scratch_shapes=[pltpu.VMEM((chunk + POOL_HALO, POOL_WIDTH), F32)],
        compiler_params=pltpu.CompilerParams(
            dimension_semantics=("parallel", "arbitrary"),
            vmem_limit_bytes=_vmem_limit(resident)),
    )(x, w_in, w_pool, pool_scale)


def _moba_kernel(q_ref, k_ref, v_ref, o_ref, qt_ref, vt_ref, kmean_ref, acc_ref,
                 s0_ref, s1_ref, bias0_ref, bias1_ref,
                 meff0_ref, meff1_ref, tmax0_ref, tmax1_ref):
    S = q_ref.shape[1]
    nb = S // MOBA_BLOCK
    blk = MOBA_BLOCK
    hd = ATTN_HEAD_DIM
    heads = range(HEADS_PER_SLAB)
    n_pairs = nb // 2
    s_refs = (s0_ref, s1_ref)
    bias_refs, meff_refs = (bias0_ref, bias1_ref), (meff0_ref, meff1_ref)
    tmax_refs = (tmax0_ref, tmax1_ref)
    trows = tmax0_ref.shape[1]

    no_chan = jnp.zeros((hd, blk), BF16)
    vrows = vt_ref.shape[1]
    ones_row = jnp.where(lax.broadcasted_iota(jnp.int32, (vrows - hd, blk), 0) == 0,
                         1.0, 0.0).astype(BF16)
    for c in range(nb):
        cols = slice(c * blk, (c + 1) * blk)
        qtb = q_ref[0, cols, :].T
        vtb = v_ref[0, cols, :].T
        for h in heads:
            qt_ref[h, :, cols] = jnp.concatenate(
                [qtb[r * hd:(r + 1) * hd, :] if r == h else no_chan for r in heads], axis=0)
            vt_ref[h, 0:hd, cols] = vtb[h * hd:(h + 1) * hd, :]
            vt_ref[h, hd:vrows, cols] = ones_row
        part = jnp.mean(k_ref[0, cols, :].astype(F32), axis=0, keepdims=True)
        for n in range(kmean_ref.shape[0]):
            kmean_ref[n, c:c + 1, :] = part.astype(BF16)
            part = part - part.astype(BF16).astype(F32)

    blk_idx = lax.broadcasted_iota(jnp.int32, (nb, blk), 0)
    tile_idx = lax.broadcasted_iota(jnp.int32, (trows, blk), 0)
    for tmax_ref in tmax_refs:
        tmax_ref[...] = jnp.full(tmax_ref.shape, NEG_INF, F32)
    causal = (lax.broadcasted_iota(jnp.int32, (blk, blk), 0)
              <= lax.broadcasted_iota(jnp.int32, (blk, blk), 1))

    def q_offsets(i):
        return [pl.multiple_of(qb * blk, blk) for qb in (i, nb - 1 - i)]

    def past_tile(i, t):
        first = t <= i
        sel = jnp.where(first, 0, 1)
        j = jnp.where(first, t - 1, t - i - 1)
        q_off = q_offsets(i)
        q0 = pl.multiple_of(jnp.where(first, q_off[0], q_off[1]), blk)
        return sel, q0, j, pl.multiple_of(j * blk, blk)

    def gates(i, bias_ref):
        never = jnp.full((trows - nb, blk), NEG_INF, F32)
        for h in heads:
            by_block = []
            for sel, qb in ((0, i), (1, nb - 1 - i)):
                q0 = pl.multiple_of(qb * blk, blk)
                qh = qt_ref[h, :, pl.ds(q0, blk)]
                gate = sum(_dot(kmean_ref[n], qh) for n in range(kmean_ref.shape[0]))
                past = blk_idx < qb
                gate = jnp.where(past, gate, NEG_INF)
                bias = jnp.full((nb, blk), NEG_INF, F32)
                for _ in range(MOBA_TOPK):
                    top = jnp.max(gate, axis=0, keepdims=True)
                    lowest = jnp.min(jnp.where(gate == top, blk_idx, nb), axis=0, keepdims=True)
                    taken = blk_idx == lowest
                    bias = jnp.where(taken, 0.0, bias)
                    gate = jnp.where(taken, NEG_INF, gate)
                bias = jnp.where(past, bias, NEG_INF)
                by_block.append(jnp.concatenate([bias, never], axis=0))
            first = pltpu.roll(by_block[0], 1, axis=0)
            second = pltpu.roll(by_block[1], i + 1, axis=0)
            own = (tile_idx == 0) | (tile_idx == nb)
            bias_ref[h] = jnp.where(own, 0.0, jnp.where(tile_idx <= i, first, second))

    def score_tile(i, t, h, s_ref, tmax_ref):
        q_off = q_offsets(i)
        if t in (0, nb):
            q0 = k0 = q_off[0 if t == 0 else 1]
        else:
            _, q0, _, k0 = past_tile(i, t)
        s = _dot(k_ref[0, pl.ds(k0, blk), :], qt_ref[h, :, pl.ds(q0, blk)])
        if t in (0, nb):
            s = jnp.where(causal, s, NEG_INF)
        s_ref[h, t] = s
        tmax_ref[h, t:t + 1, :] = jnp.max(s, axis=0, keepdims=True)

    def chosen_max(i, tmax_ref, bias_ref, meff_ref):
        in_first = tile_idx <= i
        in_second = jnp.logical_and(tile_idx > i, tile_idx <= nb)
        for h in heads:
            bias = bias_ref[h]
            tm = tmax_ref[h] + bias
            m_first = jnp.max(jnp.where(in_first, tm, NEG_INF), axis=0, keepdims=True)
            m_second = jnp.max(jnp.where(in_second, tm, NEG_INF), axis=0, keepdims=True)
            meff_ref[h] = jnp.where(in_first, m_first, m_second) - bias

    def value_tile(i, t, h, s_ref, meff_ref):
        if t in (0, nb):
            sel = 0 if t == 0 else 1
            k0 = q_offsets(i)[sel]
        else:
            sel, _, _, k0 = past_tile(i, t)
        p = jnp.exp2(s_ref[h, t] - meff_ref[h, t:t + 1, :]).astype(BF16)
        acc_ref[sel, h] += _dot(vt_ref[h, :, pl.ds(k0, blk)], p)

    def store(i):
        q_off = q_offsets(i)
        for sel in range(2):
            o_t = jnp.concatenate(
                [acc_ref[sel, h, 0:hd, :] * (1.0 / acc_ref[sel, h, hd:hd + 1, :]) for h in heads],
                axis=0)
            o_ref[0, pl.ds(q_off[sel], blk), :] = o_t.T.astype(BF16)

    tile_order = [0, nb] + list(range(1, nb))

    def pipeline_step(i, par, with_scores=True):
        cur, nxt = par, 1 - par
        chosen_max(i, tmax_refs[cur], bias_refs[cur], meff_refs[cur])
        acc_ref[...] = jnp.zeros(acc_ref.shape, F32)
        for n, t in enumerate(tile_order):
            for h in heads:
                value_tile(i, t, h, s_refs[cur], meff_refs[cur])
                if with_scores:
                    score_tile(i + 1, t, h, s_refs[nxt], tmax_refs[nxt])
            if with_scores and n == 0:
                gates(i + 1, bias_refs[nxt])
        store(i)

    gates(0, bias_refs[0])
    for t in tile_order:
        for h in heads:
            score_tile(0, t, h, s_refs[0], tmax_refs[0])

    def two_pairs(d, carry):
        pipeline_step(2 * d, 0)
        pipeline_step(2 * d + 1, 1)
        return carry

    lax.fori_loop(0, n_pairs // 2 - 1, two_pairs, 0)
    pipeline_step(n_pairs - 2, 0)
    pipeline_step(n_pairs - 1, 1, with_scores=False)


def _moba(q, k, v):
    B, S, aw = q.shape
    nb = S // MOBA_BLOCK
    blk, hps, hd = MOBA_BLOCK, HEADS_PER_SLAB, ATTN_HEAD_DIM
    assert nb % 4 == 0
    trows = -(-(nb + 1) // V7X_SUBLANES) * V7X_SUBLANES
    slab = pl.BlockSpec((1, S, V7X_LANES), lambda b, g: (b, 0, g))
    per_parity = [
        pltpu.VMEM((hps, nb + 1, blk, blk), F32),
        pltpu.VMEM((hps, trows, blk), F32),
        pltpu.VMEM((hps, trows, blk), F32),
        pltpu.VMEM((hps, trows, blk), F32),
    ]
    scratch = [
        pltpu.VMEM((hps, V7X_LANES, S), BF16),
        pltpu.VMEM((hps, hd + V7X_BF16_ROWS, S), BF16),
        pltpu.VMEM((F32_AS_BF16_PARTS, nb, V7X_LANES), BF16),
        pltpu.VMEM((2, hps, hd + V7X_BF16_ROWS, blk), F32),
    ] + [spec for spec in per_parity for _ in range(2)]
    resident = (8 * S * V7X_LANES * 2 + (hps + 1) * S * V7X_LANES * 2
                + 2 * hps * (nb + 1) * blk * blk * 4 + 8 * blk * blk * 4)
    return pl.pallas_call(
        _moba_kernel,
        grid=(B, aw // V7X_LANES),
        in_specs=[slab, slab, slab],
        out_specs=slab,
        out_shape=jax.ShapeDtypeStruct((B, S, aw), BF16),
        scratch_shapes=scratch,
        compiler_params=pltpu.CompilerParams(
            dimension_semantics=("parallel", "parallel"),
            vmem_limit_bytes=_vmem_limit(resident)),
    )(q, k, v)


def _mem_fold_kernel(mem_ref, w_kv_ref, w_q_ref, w_o_ref, qk_ref, vo_ref, *, q_scale):
    M, D = mem_ref.shape[1], mem_ref.shape[2]
    xhd = D // XATTN_HEADS
    mb = mem_ref[0].astype(BF16)
    k = _dot(mb, w_kv_ref[:, 0:D]).astype(BF16)
    v = _dot(mb, w_kv_ref[:, D:2 * D]).astype(BF16)
    for h in range(XATTN_HEADS):
        cols = slice(h * xhd, (h + 1) * xhd)
        qk = lax.dot_general(w_q_ref[:, cols], k[:, cols], (((1,), (1,)), ((), ())),
                             preferred_element_type=F32)
        qk_ref[0, :, h * M:(h + 1) * M] = (qk * q_scale).astype(BF16)
        vo_ref[0, h * M:(h + 1) * M, :] = _dot(v[:, cols], w_o_ref[cols, :]).astype(BF16)


def _mem_fold(mem, w_xkv, w_xq, w_xo):
    B, M, D = mem.shape
    q_scale = (D // XATTN_HEADS) ** -0.5 * LOG2E
    hm = XATTN_HEADS * M
    resident = 2 * M * D * 4 + 4 * D * D * 2 + 4 * D * hm * 2 + 4 * M * D * 4
    return pl.pallas_call(
        functools.partial(_mem_fold_kernel, q_scale=q_scale),
        grid=(B,),
        in_specs=[pl.BlockSpec((1, M, D), lambda b: (b, 0, 0)), _const_spec(w_xkv.shape),
                  _const_spec(w_xq.shape), _const_spec(w_xo.shape)],
        out_specs=[pl.BlockSpec((1, D, hm), lambda b: (b, 0, 0)),
                   pl.BlockSpec((1, hm, D), lambda b: (b, 0, 0))],
        out_shape=[jax.ShapeDtypeStruct((B, D, hm), BF16), jax.ShapeDtypeStruct((B, hm, D), BF16)],
        compiler_params=pltpu.CompilerParams(
            dimension_semantics=("parallel",), vmem_limit_bytes=_vmem_limit(resident)),
    )(mem, w_xkv, w_xq, w_xo)


def _mix_xattn_kernel(x_ref, pool_ref, attn_ref, w_out_ref, g1_ref, b1_ref,
                      qk_ref, vo_ref, g2_ref, b2_ref, h_ref):
    D = x_ref.shape[2]
    pw = pool_ref.shape[2]
    M = qk_ref.shape[2] // XATTN_HEADS
    mix = _dot(pool_ref[0], w_out_ref[0:pw, :]) + _dot(attn_ref[0], w_out_ref[pw:D, :])
    h1 = _layer_norm(DEEPNORM_ALPHA * x_ref[0] + mix, g1_ref[...], b1_ref[...])

    s = _dot(h1.astype(BF16), qk_ref[0])
    probs = []
    for h in range(XATTN_HEADS):
        sh = s[:, h * M:(h + 1) * M]
        p = jnp.exp2(sh - jnp.max(sh, axis=-1, keepdims=True))
        probs.append((p * (1.0 / jnp.sum(p, axis=-1, keepdims=True))).astype(BF16))
    xo = _dot(jnp.concatenate(probs, axis=-1), vo_ref[0])
    h_ref[0] = _layer_norm(DEEPNORM_ALPHA * h1 + xo, g2_ref[...], b2_ref[...])


def _mix_xattn(x, pool, attn, w_out, g1, b1, qk, vo, g2, b2, *, ts):
    B, S, D = x.shape
    hm = qk.shape[2]
    tile = lambda width: pl.BlockSpec((1, ts, width), lambda b, s: (b, s, 0))
    vec = _const_spec((1, D))
    resident = (4 * ts * D * 4 + 4 * ts * pool.shape[2] * 2 + D * D * 2 + 4 * D * hm * 2
                + 6 * ts * D * 4)
    return pl.pallas_call(
        _mix_xattn_kernel,
        grid=(B, S // ts),
        in_specs=[tile(D), tile(pool.shape[2]), tile(attn.shape[2]), _const_spec(w_out.shape),
                  vec, vec, pl.BlockSpec((1, D, hm), lambda b, s: (b, 0, 0)),
                  pl.BlockSpec((1, hm, D), lambda b, s: (b, 0, 0)), vec, vec],
        out_specs=tile(D),
        out_shape=jax.ShapeDtypeStruct((B, S, D), F32),
        compiler_params=pltpu.CompilerParams(
            dimension_semantics=("parallel", "parallel"),
            vmem_limit_bytes=_vmem_limit(resident)),
    )(x, pool, attn, w_out, g1, b1, qk, vo, g2, b2)


def _ffn_kernel(h_ref, w_gate_ref, w_up_ref, w_down_ref, g_ref, b_ref, o_ref, *, ff_chunk):
    h = h_ref[...]
    hb = h.astype(BF16)
    d_ff = w_gate_ref.shape[1]
    y = jnp.zeros(h.shape, F32)
    for c in range(d_ff // ff_chunk):
        cols = slice(c * ff_chunk, (c + 1) * ff_chunk)
        gate = _dot(hb, w_gate_ref[:, cols])
        up = _dot(hb, w_up_ref[:, cols])
        act = gate * (1.0 / (1.0 + jnp.exp(-gate))) * up
        y = y + _dot(act.astype(BF16), w_down_ref[cols, :])
    o_ref[...] = _layer_norm(DEEPNORM_ALPHA * h + y, g_ref[...], b_ref[...])


def _ffn(h, w_gate, w_up, w_down, g, b, *, ts, ff_chunk):
    N, D = h.shape
    d_ff = w_gate.shape[1]
    assert N % ts == 0 and d_ff % ff_chunk == 0
    tile = pl.BlockSpec((ts, D), lambda t: (t, 0))
    vec = _const_spec((1, D))
    resident = 4 * ts * D * 4 + 3 * D * d_ff * 2 + 3 * ts * ff_chunk * 4 + 2 * ts * D * 4
    return pl.pallas_call(
        functools.partial(_ffn_kernel, ff_chunk=ff_chunk),
        grid=(N // ts,),
        in_specs=[tile, _const_spec(w_gate.shape), _const_spec(w_up.shape),
                  _const_spec(w_down.shape), vec, vec],
        out_specs=tile,
        out_shape=jax.ShapeDtypeStruct((N, D), F32),
        compiler_params=pltpu.CompilerParams(
            dimension_semantics=("parallel",), vmem_limit_bytes=_vmem_limit(resident)),
    )(h, w_gate, w_up, w_down, g, b)


def kernel(x, mem, w_in, w_pool, pool_scale, w_out, ln1_g, ln1_b, w_xq, w_xkv, w_xo,
           ln2_g, ln2_b, w_gate, w_up, w_down, ln3_g, ln3_b):
    B, S, D = x.shape
    assert w_in.shape[0] == DEPTH and S % (4 * MOBA_BLOCK) == 0
    assert (w_in.shape[2] - POOL_WIDTH) % (3 * V7X_LANES) == 0
    h = x
    for l in range(DEPTH):
        wb = lambda w: w[l].astype(BF16)
        row = lambda p: p[l].reshape(1, -1)
        pool, q, k, v = _proj_pool(h, wb(w_in), wb(w_pool), row(pool_scale), ts=1024,
                                   chunk=512)
        attn = _moba(q, k, v)
        qk, vo = _mem_fold(mem, wb(w_xkv), wb(w_xq), wb(w_xo))
        h = _mix_xattn(h, pool, attn, wb(w_out), row(ln1_g), row(ln1_b), qk, vo,
                       row(ln2_g), row(ln2_b), ts=1024)
        h = _ffn(h.reshape(B * S, D), wb(w_gate), wb(w_up), wb(w_down),
                 row(ln3_g), row(ln3_b), ts=1024, ff_chunk=256).reshape(B, S, D)
    return h
```

```python
import functools
import math

import jax
import jax.numpy as jnp
from jax import lax
from jax.experimental import pallas as pl
from jax.experimental.pallas import tpu as pltpu

POOL_WINDOWS = (2, 4, 8, 16)
POOL_GROUP = 128
POOL_WIDTH = POOL_GROUP * len(POOL_WINDOWS)
POOL_HALO = max(POOL_WINDOWS)
ATTN_HEAD_DIM = 64
MOBA_BLOCK = 256
MOBA_TOPK = 3
XATTN_HEADS = 4
DEPTH = 1
DEEPNORM_ALPHA = (2.0 * DEPTH) ** 0.25
LN_EPS = 1e-5
LOG2E = math.log2(math.e)

V7X_LANES = 128
V7X_SUBLANES = 8
V7X_VMEM_BYTES = 64 * 1024 * 1024
V7X_BF16_ROWS = 16
HEADS_PER_SLAB = V7X_LANES // ATTN_HEAD_DIM

F32 = jnp.float32
BF16 = jnp.bfloat16
NEG_INF = float("-inf")


def _vmem_limit(resident_bytes):
    return int(min(2 * resident_bytes, V7X_VMEM_BYTES * 7 // 8))


def _const_spec(shape):
    return pl.BlockSpec(shape, lambda *_: (0,) * len(shape), pipeline_mode=pl.Buffered(1))


def _dot(a, b):
    return jnp.dot(a, b, preferred_element_type=F32)


def _layer_norm(y, g, b):
    mu = jnp.mean(y, axis=-1, keepdims=True)
    d = y - mu
    var = jnp.mean(d * d, axis=-1, keepdims=True)
    return d * lax.rsqrt(var + LN_EPS) * g + b


def _proj_pool_kernel(x_ref, w_in_ref, w_pool_ref, scale_ref,
                      pool_ref, q_ref, k_ref, v_ref, ubuf_ref, *, q_scale):
    s_idx = pl.program_id(1)
    ts = x_ref.shape[1]
    aw = q_ref.shape[2]
    chunk = ubuf_ref.shape[0] - POOL_HALO

    @pl.when(s_idx == 0)
    def _():
        ubuf_ref[0:POOL_HALO, :] = jnp.zeros((POOL_HALO, POOL_WIDTH), F32)

    for c in range(ts // chunk):
        rows = slice(c * chunk, (c + 1) * chunk)
        xb = x_ref[0, rows, :].astype(BF16)
        u = _dot(xb, w_in_ref[:, 0:POOL_WIDTH])
        ubuf_ref[POOL_HALO:POOL_HALO + chunk, :] = u
        head_pos = (s_idx * ts + c * chunk
                    + lax.broadcasted_iota(jnp.int32, (POOL_HALO, POOL_GROUP), 0))
        for g, w in enumerate(POOL_WINDOWS):
            cols = slice(g * POOL_GROUP, (g + 1) * POOL_GROUP)
            win = ubuf_ref[:, cols]
            span = 1
            while span < w:
                win = win + pltpu.roll(win, span, axis=0)
                span *= 2
            win = win[POOL_HALO:, :]
            head_inv = 1.0 / jnp.minimum(head_pos + 1, w).astype(F32)
            mean = jnp.concatenate(
                [win[0:POOL_HALO, :] * head_inv, win[POOL_HALO:, :] * (1.0 / w)], axis=0)
            mixed = _dot((mean - u[:, cols]).astype(BF16), w_pool_ref[g]) * scale_ref[:, cols]
            pool_ref[0, rows, cols] = mixed.astype(BF16)
        ubuf_ref[0:POOL_HALO, :] = u[chunk - POOL_HALO:, :]
        q_ref[0, rows, :] = (_dot(xb, w_in_ref[:, POOL_WIDTH:POOL_WIDTH + aw]) * q_scale).astype(BF16)
        k_ref[0, rows, :] = _dot(xb, w_in_ref[:, POOL_WIDTH + aw:POOL_WIDTH + 2 * aw]).astype(BF16)
        v_ref[0, rows, :] = _dot(xb, w_in_ref[:, POOL_WIDTH + 2 * aw:POOL_WIDTH + 3 * aw]).astype(BF16)


def _proj_pool(x, w_in, w_pool, pool_scale, *, ts, chunk):
    B, S, D = x.shape
    aw = (w_in.shape[1] - POOL_WIDTH) // 3
    q_scale = ATTN_HEAD_DIM ** -0.5 * LOG2E
    slab = lambda width: pl.BlockSpec((1, ts, width), lambda b, s: (b, s, 0))
    assert ts % chunk == 0 and chunk > POOL_HALO
    resident = (2 * ts * D * 4 + D * w_in.shape[1] * 2 + 2 * ts * (POOL_WIDTH + 3 * aw) * 2
                + (chunk + POOL_HALO) * POOL_WIDTH * 4 + 4 * chunk * w_in.shape[1] * 4)
    return pl.pallas_call(
        functools.partial(_proj_pool_kernel, q_scale=q_scale),
        grid=(B, S // ts),
        in_specs=[slab(D), _const_spec(w_in.shape), _const_spec(w_pool.shape),
                  _const_spec(pool_scale.shape)],
        out_specs=[slab(POOL_WIDTH), slab(aw), slab(aw), slab(aw)],
        out_shape=[jax.ShapeDtypeStruct((B, S, POOL_WIDTH), BF16)]
        + [jax.ShapeDtypeStruct((B, S, aw), BF16)] * 3,
        scratch_shapes=[pltpu.VMEM((chunk + POOL_HALO, POOL_WIDTH), F32)],
        compiler_params=pltpu.CompilerParams(
            dimension_semantics=("parallel", "arbitrary"),
            vmem_limit_bytes=_vmem_limit(resident)),
    )(x, w_in, w_pool, pool_scale)


def _moba_kernel(q_ref, k_ref, v_ref, o_ref, qt_ref, vt_ref, kmean_ref, acc_ref,
                 s0_ref, s1_ref, bias0_ref, bias1_ref,
                 meff0_ref, meff1_ref, tmax0_ref, tmax1_ref):
    S = q_ref.shape[1]
    nb = S // MOBA_BLOCK
    blk = MOBA_BLOCK
    hd = ATTN_HEAD_DIM
    heads = range(HEADS_PER_SLAB)
    n_pairs = nb // 2
    s_refs = (s0_ref, s1_ref)
    bias_refs, meff_refs = (bias0_ref, bias1_ref), (meff0_ref, meff1_ref)
    tmax_refs = (tmax0_ref, tmax1_ref)
    trows = tmax0_ref.shape[1]

    chan_head = lax.broadcasted_iota(jnp.int32, (V7X_LANES, blk), 0) // hd
    vrows = vt_ref.shape[1]
    ones_row = jnp.where(lax.broadcasted_iota(jnp.int32, (vrows - hd, blk), 0) == 0,
                         1.0, 0.0).astype(BF16)
    for c in range(nb):
        cols = slice(c * blk, (c + 1) * blk)
        qtf = q_ref[0, cols, :].astype(F32).T
        for h in heads:
            qt_ref[h, :, cols] = jnp.where(chan_head == h, qtf, 0.0).astype(BF16)
        vtb = v_ref[0, cols, :].astype(F32).T.astype(BF16)
        for h in heads:
            vt_ref[h, 0:hd, cols] = vtb[h * hd:(h + 1) * hd, :]
            vt_ref[h, hd:vrows, cols] = ones_row
        kmean_ref[c:c + 1, :] = jnp.mean(k_ref[0, cols, :].astype(F32), axis=0, keepdims=True)

    blk_idx = lax.broadcasted_iota(jnp.int32, (nb, blk), 0)
    tile_idx = lax.broadcasted_iota(jnp.int32, (trows, blk), 0)
    for tmax_ref in tmax_refs:
        tmax_ref[...] = jnp.full(tmax_ref.shape, NEG_INF, F32)
    causal = (lax.broadcasted_iota(jnp.int32, (blk, blk), 0)
              <= lax.broadcasted_iota(jnp.int32, (blk, blk), 1))

    def q_offsets(i):
        return [pl.multiple_of(qb * blk, blk) for qb in (i, nb - 1 - i)]

    def past_tile(i, t):
        first = t <= i
        sel = jnp.where(first, 0, 1)
        j = jnp.where(first, t - 1, t - i - 1)
        q_off = q_offsets(i)
        q0 = pl.multiple_of(jnp.where(first, q_off[0], q_off[1]), blk)
        return sel, q0, j, pl.multiple_of(j * blk, blk)

    def gates(i, bias_ref):
        never = jnp.full((trows - nb, blk), NEG_INF, F32)
        for h in heads:
            by_block = []
            for sel, qb in ((0, i), (1, nb - 1 - i)):
                q0 = pl.multiple_of(qb * blk, blk)
                qhf = qt_ref[h, :, pl.ds(q0, blk)].astype(F32)
                gate = jnp.dot(kmean_ref[...], qhf, preferred_element_type=F32,
                               precision=lax.Precision.HIGHEST)
                past = blk_idx < qb
                gate = jnp.where(past, gate, NEG_INF)
                bias = jnp.full((nb, blk), NEG_INF, F32)
                for _ in range(MOBA_TOPK):
                    top = jnp.max(gate, axis=0, keepdims=True)
                    lowest = jnp.min(jnp.where(gate == top, blk_idx, nb), axis=0, keepdims=True)
                    taken = blk_idx == lowest
                    bias = jnp.where(taken, 0.0, bias)
                    gate = jnp.where(taken, NEG_INF, gate)
                bias = jnp.where(past, bias, NEG_INF)
                by_block.append(jnp.concatenate([bias, never], axis=0))
            first = pltpu.roll(by_block[0], 1, axis=0)
            second = pltpu.roll(by_block[1], i + 1, axis=0)
            own = (tile_idx == 0) | (tile_idx == nb)
            bias_ref[h] = jnp.where(own, 0.0, jnp.where(tile_idx <= i, first, second))

    def score_tile(i, t, h, s_ref, tmax_ref):
        q_off = q_offsets(i)
        if t in (0, nb):
            q0 = k0 = q_off[0 if t == 0 else 1]
        else:
            _, q0, _, k0 = past_tile(i, t)
        s = _dot(k_ref[0, pl.ds(k0, blk), :], qt_ref[h, :, pl.ds(q0, blk)])
        if t in (0, nb):
            s = jnp.where(causal, s, NEG_INF)
        s_ref[h, t] = s
        tmax_ref[h, t:t + 1, :] = jnp.max(s, axis=0, keepdims=True)

    def chosen_max(i, tmax_ref, bias_ref, meff_ref):
        in_first = tile_idx <= i
        in_second = jnp.logical_and(tile_idx > i, tile_idx <= nb)
        for h in heads:
            bias = bias_ref[h]
            tm = tmax_ref[h] + bias
            m_first = jnp.max(jnp.where(in_first, tm, NEG_INF), axis=0, keepdims=True)
            m_second = jnp.max(jnp.where(in_second, tm, NEG_INF), axis=0, keepdims=True)
            meff_ref[h] = jnp.where(in_first, m_first, m_second) - bias

    def value_tile(i, t, h, s_ref, meff_ref):
        if t in (0, nb):
            sel = 0 if t == 0 else 1
            k0 = q_offsets(i)[sel]
        else:
            sel, _, _, k0 = past_tile(i, t)
        p = jnp.exp2(s_ref[h, t] - meff_ref[h, t:t + 1, :]).astype(BF16)
        acc_ref[sel, h] += _dot(vt_ref[h, :, pl.ds(k0, blk)], p)

    def store(i):
        q_off = q_offsets(i)
        for sel in range(2):
            o_t = jnp.concatenate(
                [acc_ref[sel, h, 0:hd, :] * (1.0 / acc_ref[sel, h, hd:hd + 1, :]) for h in heads],
                axis=0)
            o_ref[0, pl.ds(q_off[sel], blk), :] = o_t.T.astype(BF16)

    tile_order = [0, nb] + list(range(1, nb))

    def pipeline_step(i, par, with_scores=True):
        cur, nxt = par, 1 - par
        chosen_max(i, tmax_refs[cur], bias_refs[cur], meff_refs[cur])
        acc_ref[...] = jnp.zeros(acc_ref.shape, F32)
        for n, t in enumerate(tile_order):
            for h in heads:
                value_tile(i, t, h, s_refs[cur], meff_refs[cur])
                if with_scores:
                    score_tile(i + 1, t, h, s_refs[nxt], tmax_refs[nxt])
            if with_scores and n == len(tile_order) // 2:
                gates(i + 1, bias_refs[nxt])
        store(i)

    gates(0, bias_refs[0])
    for t in tile_order:
        for h in heads:
            score_tile(0, t, h, s_refs[0], tmax_refs[0])

    def one_pair(i, carry):
        pl.when(i % 2 == 0)(lambda: pipeline_step(i, 0))
        pl.when(i % 2 == 1)(lambda: pipeline_step(i, 1))
        return carry

    lax.fori_loop(0, n_pairs - 1, one_pair, 0)
    pipeline_step(n_pairs - 1, (n_pairs - 1) % 2, with_scores=False)


def _moba(q, k, v):
    B, S, aw = q.shape
    nb = S // MOBA_BLOCK
    blk, hps, hd = MOBA_BLOCK, HEADS_PER_SLAB, ATTN_HEAD_DIM
    assert nb % 4 == 0
    trows = -(-(nb + 1) // V7X_SUBLANES) * V7X_SUBLANES
    slab = pl.BlockSpec((1, S, V7X_LANES), lambda b, g: (b, 0, g))
    per_parity = [
        pltpu.VMEM((hps, nb + 1, blk, blk), F32),
        pltpu.VMEM((hps, trows, blk), F32),
        pltpu.VMEM((hps, trows, blk), F32),
        pltpu.VMEM((hps, trows, blk), F32),
    ]
    scratch = [
        pltpu.VMEM((hps, V7X_LANES, S), BF16),
        pltpu.VMEM((hps, hd + V7X_BF16_ROWS, S), BF16),
        pltpu.VMEM((nb, V7X_LANES), F32),
        pltpu.VMEM((2, hps, hd + V7X_BF16_ROWS, blk), F32),
    ] + [spec for spec in per_parity for _ in range(2)]
    resident = (8 * S * V7X_LANES * 2 + (hps + 1) * S * V7X_LANES * 2
                + 2 * hps * (nb + 1) * blk * blk * 4 + 8 * blk * blk * 4)
    return pl.pallas_call(
        _moba_kernel,
        grid=(B, aw // V7X_LANES),
        in_specs=[slab, slab, slab],
        out_specs=slab,
        out_shape=jax.ShapeDtypeStruct((B, S, aw), BF16),
        scratch_shapes=scratch,
        compiler_params=pltpu.CompilerParams(
            dimension_semantics=("parallel", "parallel"),
            vmem_limit_bytes=_vmem_limit(resident)),
    )(q, k, v)


def _mem_fold_kernel(mem_ref, w_kv_ref, w_q_ref, w_o_ref, qk_ref, vo_ref, *, q_scale):
    M, D = mem_ref.shape[1], mem_ref.shape[2]
    xhd = D // XATTN_HEADS
    mb = mem_ref[0].astype(BF16)
    k = _dot(mb, w_kv_ref[:, 0:D]).astype(BF16)
    v = _dot(mb, w_kv_ref[:, D:2 * D]).astype(BF16)
    for h in range(XATTN_HEADS):
        cols = slice(h * xhd, (h + 1) * xhd)
        qk = lax.dot_general(w_q_ref[:, cols], k[:, cols], (((1,), (1,)), ((), ())),
                             preferred_element_type=F32)
        qk_ref[0, :, h * M:(h + 1) * M] = (qk * q_scale).astype(BF16)
        vo_ref[0, h * M:(h + 1) * M, :] = _dot(v[:, cols], w_o_ref[cols, :]).astype(BF16)


def _mem_fold(mem, w_xkv, w_xq, w_xo):
    B, M, D = mem.shape
    q_scale = (D // XATTN_HEADS) ** -0.5 * LOG2E
    hm = XATTN_HEADS * M
    resident = 2 * M * D * 4 + 4 * D * D * 2 + 4 * D * hm * 2 + 4 * M * D * 4
    return pl.pallas_call(
        functools.partial(_mem_fold_kernel, q_scale=q_scale),
        grid=(B,),
        in_specs=[pl.BlockSpec((1, M, D), lambda b: (b, 0, 0)), _const_spec(w_xkv.shape),
                  _const_spec(w_xq.shape), _const_spec(w_xo.shape)],
        out_specs=[pl.BlockSpec((1, D, hm), lambda b: (b, 0, 0)),
                   pl.BlockSpec((1, hm, D), lambda b: (b, 0, 0))],
        out_shape=[jax.ShapeDtypeStruct((B, D, hm), BF16), jax.ShapeDtypeStruct((B, hm, D), BF16)],
        compiler_params=pltpu.CompilerParams(
            dimension_semantics=("parallel",), vmem_limit_bytes=_vmem_limit(resident)),
    )(mem, w_xkv, w_xq, w_xo)


def _mix_xattn_kernel(x_ref, pool_ref, attn_ref, w_out_ref, g1_ref, b1_ref,
                      qk_ref, vo_ref, g2_ref, b2_ref, h_ref):
    D = x_ref.shape[2]
    pw = pool_ref.shape[2]
    M = qk_ref.shape[2] // XATTN_HEADS
    mix = _dot(pool_ref[0], w_out_ref[0:pw, :]) + _dot(attn_ref[0], w_out_ref[pw:D, :])
    h1 = _layer_norm(DEEPNORM_ALPHA * x_ref[0] + mix, g1_ref[...], b1_ref[...])

    s = _dot(h1.astype(BF16), qk_ref[0])
    probs = []
    for h in range(XATTN_HEADS):
        sh = s[:, h * M:(h + 1) * M]
        p = jnp.exp2(sh - jnp.max(sh, axis=-1, keepdims=True))
        probs.append((p * (1.0 / jnp.sum(p, axis=-1, keepdims=True))).astype(BF16))
    xo = _dot(jnp.concatenate(probs, axis=-1), vo_ref[0])
    h_ref[0] = _layer_norm(DEEPNORM_ALPHA * h1 + xo, g2_ref[...], b2_ref[...])


def _mix_xattn(x, pool, attn, w_out, g1, b1, qk, vo, g2, b2, *, ts):
    B, S, D = x.shape
    hm = qk.shape[2]
    tile = lambda width: pl.BlockSpec((1, ts, width), lambda b, s: (b, s, 0))
    vec = _const_spec((1, D))
    resident = (4 * ts * D * 4 + 4 * ts * pool.shape[2] * 2 + D * D * 2 + 4 * D * hm * 2
                + 6 * ts * D * 4)
    return pl.pallas_call(
        _mix_xattn_kernel,
        grid=(B, S // ts),
        in_specs=[tile(D), tile(pool.shape[2]), tile(attn.shape[2]), _const_spec(w_out.shape),
                  vec, vec, pl.BlockSpec((1, D, hm), lambda b, s: (b, 0, 0)),
                  pl.BlockSpec((1, hm, D), lambda b, s: (b, 0, 0)), vec, vec],
        out_specs=tile(D),
        out_shape=jax.ShapeDtypeStruct((B, S, D), F32),
        compiler_params=pltpu.CompilerParams(
            dimension_semantics=("parallel", "parallel"),
            vmem_limit_bytes=_vmem_limit(resident)),
    )(x, pool, attn, w_out, g1, b1, qk, vo, g2, b2)


def _ffn_kernel(h_ref, w_gate_ref, w_up_ref, w_down_ref, g_ref, b_ref, o_ref, *, ff_chunk):
    h = h_ref[...]
    hb = h.astype(BF16)
    d_ff = w_gate_ref.shape[1]
    y = jnp.zeros(h.shape, F32)
    for c in range(d_ff // ff_chunk):
        cols = slice(c * ff_chunk, (c + 1) * ff_chunk)
        gate = _dot(hb, w_gate_ref[:, cols])
        up = _dot(hb, w_up_ref[:, cols])
        act = gate * (1.0 / (1.0 + jnp.exp(-gate))) * up
        y = y + _dot(act.astype(BF16), w_down_ref[cols, :])
    o_ref[...] = _layer_norm(DEEPNORM_ALPHA * h + y, g_ref[...], b_ref[...])


def _ffn(h, w_gate, w_up, w_down, g, b, *, ts, ff_chunk):
    N, D = h.shape
    d_ff = w_gate.shape[1]
    assert N % ts == 0 and d_ff % ff_chunk == 0
    tile = pl.BlockSpec((ts, D), lambda t: (t, 0))
    vec = _const_spec((1, D))
    resident = 4 * ts * D * 4 + 3 * D * d_ff * 2 + 3 * ts * ff_chunk * 4 + 2 * ts * D * 4
    return pl.pallas_call(
        functools.partial(_ffn_kernel, ff_chunk=ff_chunk),
        grid=(N // ts,),
        in_specs=[tile, _const_spec(w_gate.shape), _const_spec(w_up.shape),
                  _const_spec(w_down.shape), vec, vec],
        out_specs=tile,
        out_shape=jax.ShapeDtypeStruct((N, D), F32),
        compiler_params=pltpu.CompilerParams(
            dimension_semantics=("parallel",), vmem_limit_bytes=_vmem_limit(resident)),
    )(h, w_gate, w_up, w_down, g, b)


def kernel(x, mem, w_in, w_pool, pool_scale, w_out, ln1_g, ln1_b, w_xq, w_xkv, w_xo,
           ln2_g, ln2_b, w_gate, w_up, w_down, ln3_g, ln3_b):
    B, S, D = x.shape
    assert w_in.shape[0] == DEPTH and S % (4 * MOBA_BLOCK) == 0
    assert (w_in.shape[2] - POOL_WIDTH) % (3 * V7X_LANES) == 0
    h = x
    for l in range(DEPTH):
        wb = lambda w: w[l].astype(BF16)
        row = lambda p: p[l].reshape(1, -1)
        pool, q, k, v = _proj_pool(h, wb(w_in), wb(w_pool), row(pool_scale), ts=1024,
                                   chunk=512)
        attn = _moba(q, k, v)
        qk, vo = _mem_fold(mem, wb(w_xkv), wb(w_xq), wb(w_xo))
        h = _mix_xattn(h, pool, attn, wb(w_out), row(ln1_g), row(ln1_b), qk, vo,
                       row(ln2_g), row(ln2_b), ts=1024)
        h = _ffn(h.reshape(B * S, D), wb(w_gate), wb(w_up), wb(w_down),
                 row(ln3_g), row(ln3_b), ts=1024, ff_chunk=256).reshape(B, S, D)
    return h
```

```python
import functools
import math

import jax
import jax.numpy as jnp
from jax import lax
from jax.experimental import pallas as pl
from jax.experimental.pallas import tpu as pltpu

POOL_WINDOWS = (2, 4, 8, 16)
POOL_GROUP = 128
POOL_WIDTH = POOL_GROUP * len(POOL_WINDOWS)
POOL_HALO = max(POOL_WINDOWS)
ATTN_HEAD_DIM = 64
MOBA_BLOCK = 256
MOBA_TOPK = 3
XATTN_HEADS = 4
DEPTH = 1
DEEPNORM_ALPHA = (2.0 * DEPTH) ** 0.25
LN_EPS = 1e-5
LOG2E = math.log2(math.e)

V7X_LANES = 128
V7X_SUBLANES = 8
V7X_VMEM_BYTES = 64 * 1024 * 1024
V7X_BF16_ROWS = 16
HEADS_PER_SLAB = V7X_LANES // ATTN_HEAD_DIM

F32 = jnp.float32
BF16 = jnp.bfloat16
NEG_INF = float("-inf")


def _vmem_limit(resident_bytes):
    return int(min(2 * resident_bytes, V7X_VMEM_BYTES * 7 // 8))


def _const_spec(shape):
    return pl.BlockSpec(shape, lambda *_: (0,) * len(shape), pipeline_mode=pl.Buffered(1))


def _dot(a, b):
    return jnp.dot(a, b, preferred_element_type=F32)


def _layer_norm(y, g, b):
    mu = jnp.mean(y, axis=-1, keepdims=True)
    d = y - mu
    var = jnp.mean(d * d, axis=-1, keepdims=True)
    return d * lax.rsqrt(var + LN_EPS) * g + b


def _proj_pool_kernel(x_ref, w_in_ref, w_pool_ref, scale_ref,
                      pool_ref, q_ref, k_ref, v_ref, ubuf_ref, *, q_scale):
    s_idx = pl.program_id(1)
    ts = x_ref.shape[1]
    aw = q_ref.shape[2]
    chunk = ubuf_ref.shape[0] - POOL_HALO

    @pl.when(s_idx == 0)
    def _():
        ubuf_ref[0:POOL_HALO, :] = jnp.zeros((POOL_HALO, POOL_WIDTH), F32)

    for c in range(ts // chunk):
        rows = slice(c * chunk, (c + 1) * chunk)
        xb = x_ref[0, rows, :].astype(BF16)
        u = _dot(xb, w_in_ref[:, 0:POOL_WIDTH])
        ubuf_ref[POOL_HALO:POOL_HALO + chunk, :] = u
        head_pos = (s_idx * ts + c * chunk
                    + lax.broadcasted_iota(jnp.int32, (POOL_HALO, POOL_GROUP), 0))
        for g, w in enumerate(POOL_WINDOWS):
            cols = slice(g * POOL_GROUP, (g + 1) * POOL_GROUP)
            win = ubuf_ref[:, cols]
            span = 1
            while span < w:
                win = win + pltpu.roll(win, span, axis=0)
                span *= 2
            win = win[POOL_HALO:, :]
            head_inv = 1.0 / jnp.minimum(head_pos + 1, w).astype(F32)
            mean = jnp.concatenate(
                [win[0:POOL_HALO, :] * head_inv, win[POOL_HALO:, :] * (1.0 / w)], axis=0)
            mixed = _dot((mean - u[:, cols]).astype(BF16), w_pool_ref[g]) * scale_ref[:, cols]
            pool_ref[0, rows, cols] = mixed.astype(BF16)
        ubuf_ref[0:POOL_HALO, :] = u[chunk - POOL_HALO:, :]
        q_ref[0, rows, :] = (_dot(xb, w_in_ref[:, POOL_WIDTH:POOL_WIDTH + aw]) * q_scale).astype(BF16)
        k_ref[0, rows, :] = _dot(xb, w_in_ref[:, POOL_WIDTH + aw:POOL_WIDTH + 2 * aw]).astype(BF16)
        v_ref[0, rows, :] = _dot(xb, w_in_ref[:, POOL_WIDTH + 2 * aw:POOL_WIDTH + 3 * aw]).astype(BF16)


def _proj_pool(x, w_in, w_pool, pool_scale, *, ts, chunk):
    B, S, D = x.shape
    aw = (w_in.shape[1] - POOL_WIDTH) // 3
    q_scale = ATTN_HEAD_DIM ** -0.5 * LOG2E
    slab = lambda width: pl.BlockSpec((1, ts, width), lambda b, s: (b, s, 0))
    assert ts % chunk == 0 and chunk > POOL_HALO
    resident = (2 * ts * D * 4 + D * w_in.shape[1] * 2 + 2 * ts * (POOL_WIDTH + 3 * aw) * 2
                + (chunk + POOL_HALO) * POOL_WIDTH * 4 + 4 * chunk * w_in.shape[1] * 4)
    return pl.pallas_call(
        functools.partial(_proj_pool_kernel, q_scale=q_scale),
        grid=(B, S // ts),
        in_specs=[slab(D), _const_spec(w_in.shape), _const_spec(w_pool.shape),
                  _const_spec(pool_scale.shape)],
        out_specs=[slab(POOL_WIDTH), slab(aw), slab(aw), slab(aw)],
        out_shape=[jax.ShapeDtypeStruct((B, S, POOL_WIDTH), BF16)]
        + [jax.ShapeDtypeStruct((B, S, aw), BF16)] * 3,
        scratch_shapes=[pltpu.VMEM((chunk + POOL_HALO, POOL_WIDTH), F32)],
        compiler_params=pltpu.CompilerParams(
            dimension_semantics=("parallel", "arbitrary"),
            vmem_limit_bytes=_vmem_limit(resident)),
    )(x, w_in, w_pool, pool_scale)


def _moba_kernel(q_ref, k_ref, v_ref, o_ref, qt_ref, vt_ref, acc_ref,
                 s0_ref, kmean_ref, s1_ref, bias0_ref, bias1_ref,
                 meff0_ref, meff1_ref, tmax0_ref, tmax1_ref):
    S = q_ref.shape[1]
    nb = S // MOBA_BLOCK
    blk = MOBA_BLOCK
    hd = ATTN_HEAD_DIM
    heads = range(HEADS_PER_SLAB)
    n_pairs = nb // 2
    s_refs = (s0_ref, s1_ref)
    bias_refs, meff_refs = (bias0_ref, bias1_ref), (meff0_ref, meff1_ref)
    tmax_refs = (tmax0_ref, tmax1_ref)
    trows = tmax0_ref.shape[1]

    chan_head = lax.broadcasted_iota(jnp.int32, (V7X_LANES, blk), 0) // hd
    vrows = vt_ref.shape[1]
    ones_row = jnp.where(lax.broadcasted_iota(jnp.int32, (vrows - hd, blk), 0) == 0,
                         1.0, 0.0).astype(BF16)
    for c in range(nb):
        cols = slice(c * blk, (c + 1) * blk)
        qtf = q_ref[0, cols, :].astype(F32).T
        for h in heads:
            qt_ref[h, :, cols] = jnp.where(chan_head == h, qtf, 0.0).astype(BF16)
        vtb = v_ref[0, cols, :].astype(F32).T.astype(BF16)
        for h in heads:
            vt_ref[h, 0:hd, cols] = vtb[h * hd:(h + 1) * hd, :]
            vt_ref[h, hd:vrows, cols] = ones_row
        kmean_ref[c:c + 1, :] = jnp.mean(k_ref[0, cols, :].astype(F32), axis=0, keepdims=True)

    blk_idx = lax.broadcasted_iota(jnp.int32, (nb, blk), 0)
    tile_idx = lax.broadcasted_iota(jnp.int32, (trows, blk), 0)
    for tmax_ref in tmax_refs:
        tmax_ref[...] = jnp.full(tmax_ref.shape, NEG_INF, F32)
    causal = (lax.broadcasted_iota(jnp.int32, (blk, blk), 0)
              <= lax.broadcasted_iota(jnp.int32, (blk, blk), 1))

    def q_offsets(i):
        return [pl.multiple_of(qb * blk, blk) for qb in (i, nb - 1 - i)]

    def past_tile(i, t):
        first = t <= i
        sel = jnp.where(first, 0, 1)
        j = jnp.where(first, t - 1, t - i - 1)
        q_off = q_offsets(i)
        q0 = pl.multiple_of(jnp.where(first, q_off[0], q_off[1]), blk)
        return sel, q0, j, pl.multiple_of(j * blk, blk)

    def gates(i, bias_ref):
        never = jnp.full((trows - nb, blk), NEG_INF, F32)
        for h in heads:
            by_block = []
            for sel, qb in ((0, i), (1, nb - 1 - i)):
                q0 = pl.multiple_of(qb * blk, blk)
                qhf = qt_ref[h, :, pl.ds(q0, blk)].astype(F32)
                gate = jnp.dot(kmean_ref[...], qhf, preferred_element_type=F32,
                               precision=lax.Precision.HIGHEST)
                past = blk_idx < qb
                gate = jnp.where(past, gate, NEG_INF)
                bias = jnp.full((nb, blk), NEG_INF, F32)
                for _ in range(MOBA_TOPK):
                    top = jnp.max(gate, axis=0, keepdims=True)
                    lowest = jnp.min(jnp.where(gate == top, blk_idx, nb), axis=0, keepdims=True)
                    taken = blk_idx == lowest
                    bias = jnp.where(taken, 0.0, bias)
                    gate = jnp.where(taken, NEG_INF, gate)
                bias = jnp.where(past, bias, NEG_INF)
                by_block.append(jnp.concatenate([bias, never], axis=0))
            first = pltpu.roll(by_block[0], 1, axis=0)
            second = pltpu.roll(by_block[1], i + 1, axis=0)
            own = (tile_idx == 0) | (tile_idx == nb)
            bias_ref[h] = jnp.where(own, 0.0, jnp.where(tile_idx <= i, first, second))

    def score_tile(i, t, h, s_ref, tmax_ref):
        q_off = q_offsets(i)
        if t in (0, nb):
            q0 = k0 = q_off[0 if t == 0 else 1]
        else:
            _, q0, _, k0 = past_tile(i, t)
        s = _dot(k_ref[0, pl.ds(k0, blk), :], qt_ref[h, :, pl.ds(q0, blk)])
        if t in (0, nb):
            s = jnp.where(causal, s, NEG_INF)
        s_ref[h, t] = s
        tmax_ref[h, t:t + 1, :] = jnp.max(s, axis=0, keepdims=True)

    def chosen_max(i, tmax_ref, bias_ref, meff_ref):
        in_first = tile_idx <= i
        in_second = jnp.logical_and(tile_idx > i, tile_idx <= nb)
        for h in heads:
            bias = bias_ref[h]
            tm = tmax_ref[h] + bias
            m_first = jnp.max(jnp.where(in_first, tm, NEG_INF), axis=0, keepdims=True)
            m_second = jnp.max(jnp.where(in_second, tm, NEG_INF), axis=0, keepdims=True)
            meff_ref[h] = jnp.where(in_first, m_first, m_second) - bias

    def value_tile(i, t, h, s_ref, meff_ref):
        if t in (0, nb):
            sel = 0 if t == 0 else 1
            k0 = q_offsets(i)[sel]
        else:
            sel, _, _, k0 = past_tile(i, t)
        p = jnp.exp2(s_ref[h, t] - meff_ref[h, t:t + 1, :]).astype(BF16)
        acc_ref[sel, h] += _dot(vt_ref[h, :, pl.ds(k0, blk)], p)

    def store(i):
        q_off = q_offsets(i)
        for sel in range(2):
            o_t = jnp.concatenate(
                [acc_ref[sel, h, 0:hd, :] * (1.0 / acc_ref[sel, h, hd:hd + 1, :]) for h in heads],
                axis=0)
            o_ref[0, pl.ds(q_off[sel], blk), :] = o_t.T.astype(BF16)

    tile_order = [0, nb] + list(range(1, nb))

    def pipeline_step(i, par, with_scores=True):
        cur, nxt = par, 1 - par
        chosen_max(i, tmax_refs[cur], bias_refs[cur], meff_refs[cur])
        acc_ref[...] = jnp.zeros(acc_ref.shape, F32)
        for n, t in enumerate(tile_order):
            for h in heads:
                value_tile(i, t, h, s_refs[cur], meff_refs[cur])
                if with_scores:
                    score_tile(i + 1, t, h, s_refs[nxt], tmax_refs[nxt])
            if with_scores and n == len(tile_order) // 2:
                gates(i + 1, bias_refs[nxt])
        store(i)

    gates(0, bias_refs[0])
    for t in tile_order:
        for h in heads:
            score_tile(0, t, h, s_refs[0], tmax_refs[0])

    def two_pairs(d, carry):
        pipeline_step(2 * d, 0)
        pipeline_step(2 * d + 1, 1)
        return carry

    lax.fori_loop(0, n_pairs // 2 - 1, two_pairs, 0)
    pipeline_step(n_pairs - 2, 0)
    pipeline_step(n_pairs - 1, 1, with_scores=False)


def _moba(q, k, v):
    B, S, aw = q.shape
    nb = S // MOBA_BLOCK
    blk, hps, hd = MOBA_BLOCK, HEADS_PER_SLAB, ATTN_HEAD_DIM
    assert nb % 4 == 0
    trows = -(-(nb + 1) // V7X_SUBLANES) * V7X_SUBLANES
    slab = pl.BlockSpec((1, S, V7X_LANES), lambda b, g: (b, 0, g))
    per_parity = [
        pltpu.VMEM((hps, nb + 1, blk, blk), F32),
        pltpu.VMEM((hps, trows, blk), F32),
        pltpu.VMEM((hps, trows, blk), F32),
        pltpu.VMEM((hps, trows, blk), F32),
    ]
    scratch = [
        pltpu.VMEM((hps, V7X_LANES, S), BF16),
        pltpu.VMEM((hps, hd + V7X_BF16_ROWS, S), BF16),
        pltpu.VMEM((2, hps, hd + V7X_BF16_ROWS, blk), F32),
        per_parity[0],
        pltpu.VMEM((nb, V7X_LANES), F32),
        per_parity[0],
    ] + [spec for spec in per_parity[1:] for _ in range(2)]
    resident = (8 * S * V7X_LANES * 2 + (hps + 1) * S * V7X_LANES * 2
                + 2 * hps * (nb + 1) * blk * blk * 4 + 8 * blk * blk * 4)
    return pl.pallas_call(
        _moba_kernel,
        grid=(B, aw // V7X_LANES),
        in_specs=[slab, slab, slab],
        out_specs=slab,
        out_shape=jax.ShapeDtypeStruct((B, S, aw), BF16),
        scratch_shapes=scratch,
        compiler_params=pltpu.CompilerParams(
            dimension_semantics=("parallel", "parallel"),
            vmem_limit_bytes=_vmem_limit(resident)),
    )(q, k, v)


def _mem_fold_kernel(mem_ref, w_kv_ref, w_q_ref, w_o_ref, qk_ref, vo_ref, *, q_scale):
    M, D = mem_ref.shape[1], mem_ref.shape[2]
    xhd = D // XATTN_HEADS
    mb = mem_ref[0].astype(BF16)
    k = _dot(mb, w_kv_ref[:, 0:D]).astype(BF16)
    v = _dot(mb, w_kv_ref[:, D:2 * D]).astype(BF16)
    for h in range(XATTN_HEADS):
        cols = slice(h * xhd, (h + 1) * xhd)
        qk = lax.dot_general(w_q_ref[:, cols], k[:, cols], (((1,), (1,)), ((), ())),
                             preferred_element_type=F32)
        qk_ref[0, :, h * M:(h + 1) * M] = (qk * q_scale).astype(BF16)
        vo_ref[0, h * M:(h + 1) * M, :] = _dot(v[:, cols], w_o_ref[cols, :]).astype(BF16)


def _mem_fold(mem, w_xkv, w_xq, w_xo):
    B, M, D = mem.shape
    q_scale = (D // XATTN_HEADS) ** -0.5 * LOG2E
    hm = XATTN_HEADS * M
    resident = 2 * M * D * 4 + 4 * D * D * 2 + 4 * D * hm * 2 + 4 * M * D * 4
    return pl.pallas_call(
        functools.partial(_mem_fold_kernel, q_scale=q_scale),
        grid=(B,),
        in_specs=[pl.BlockSpec((1, M, D), lambda b: (b, 0, 0)), _const_spec(w_xkv.shape),
                  _const_spec(w_xq.shape), _const_spec(w_xo.shape)],
        out_specs=[pl.BlockSpec((1, D, hm), lambda b: (b, 0, 0)),
                   pl.BlockSpec((1, hm, D), lambda b: (b, 0, 0))],
        out_shape=[jax.ShapeDtypeStruct((B, D, hm), BF16), jax.ShapeDtypeStruct((B, hm, D), BF16)],
        compiler_params=pltpu.CompilerParams(
            dimension_semantics=("parallel",), vmem_limit_bytes=_vmem_limit(resident)),
    )(mem, w_xkv, w_xq, w_xo)


def _mix_xattn_kernel(x_ref, pool_ref, attn_ref, w_out_ref, g1_ref, b1_ref,
                      qk_ref, vo_ref, g2_ref, b2_ref, h_ref):
    D = x_ref.shape[2]
    pw = pool_ref.shape[2]
    M = qk_ref.shape[2] // XATTN_HEADS
    mix = _dot(pool_ref[0], w_out_ref[0:pw, :]) + _dot(attn_ref[0], w_out_ref[pw:D, :])
    h1 = _layer_norm(DEEPNORM_ALPHA * x_ref[0] + mix, g1_ref[...], b1_ref[...])

    s = _dot(h1.astype(BF16), qk_ref[0])
    probs = []
    for h in range(XATTN_HEADS):
        sh = s[:, h * M:(h + 1) * M]
        p = jnp.exp2(sh - jnp.max(sh, axis=-1, keepdims=True))
        probs.append((p * (1.0 / jnp.sum(p, axis=-1, keepdims=True))).astype(BF16))
    xo = _dot(jnp.concatenate(probs, axis=-1), vo_ref[0])
    h_ref[0] = _layer_norm(DEEPNORM_ALPHA * h1 + xo, g2_ref[...], b2_ref[...])


def _mix_xattn(x, pool, attn, w_out, g1, b1, qk, vo, g2, b2, *, ts):
    B, S, D = x.shape
    hm = qk.shape[2]
    tile = lambda width: pl.BlockSpec((1, ts, width), lambda b, s: (b, s, 0))
    vec = _const_spec((1, D))
    resident = (4 * ts * D * 4 + 4 * ts * pool.shape[2] * 2 + D * D * 2 + 4 * D * hm * 2
                + 6 * ts * D * 4)
    return pl.pallas_call(
        _mix_xattn_kernel,
        grid=(B, S // ts),
        in_specs=[tile(D), tile(pool.shape[2]), tile(attn.shape[2]), _const_spec(w_out.shape),
                  vec, vec, pl.BlockSpec((1, D, hm), lambda b, s: (b, 0, 0)),
                  pl.BlockSpec((1, hm, D), lambda b, s: (b, 0, 0)), vec, vec],
        out_specs=tile(D),
        out_shape=jax.ShapeDtypeStruct((B, S, D), F32),
        compiler_params=pltpu.CompilerParams(
            dimension_semantics=("parallel", "parallel"),
            vmem_limit_bytes=_vmem_limit(resident)),
    )(x, pool, attn, w_out, g1, b1, qk, vo, g2, b2)


def _ffn_kernel(h_ref, w_gate_ref, w_up_ref, w_down_ref, g_ref, b_ref, o_ref, *, ff_chunk):
    h = h_ref[...]
    hb = h.astype(BF16)
    d_ff = w_gate_ref.shape[1]
    y = jnp.zeros(h.shape, F32)
    for c in range(d_ff // ff_chunk):
        cols = slice(c * ff_chunk, (c + 1) * ff_chunk)
        gate = _dot(hb, w_gate_ref[:, cols])
        up = _dot(hb, w_up_ref[:, cols])
        act = gate * (1.0 / (1.0 + jnp.exp(-gate))) * up
        y = y + _dot(act.astype(BF16), w_down_ref[cols, :])
    o_ref[...] = _layer_norm(DEEPNORM_ALPHA * h + y, g_ref[...], b_ref[...])


def _ffn(h, w_gate, w_up, w_down, g, b, *, ts, ff_chunk):
    N, D = h.shape
    d_ff = w_gate.shape[1]
    tile = pl.BlockSpec((ts, D), lambda t: (t, 0))
    vec = _const_spec((1, D))
    resident = 4 * ts * D * 4 + 3 * D * d_ff * 2 + 3 * ts * ff_chunk * 4 + 2 * ts * D * 4
    return pl.pallas_call(
        functools.partial(_ffn_kernel, ff_chunk=ff_chunk),
        grid=(N // ts,),
        in_specs=[tile, _const_spec(w_gate.shape), _const_spec(w_up.shape),
                  _const_spec(w_down.shape), vec, vec],
        out_specs=tile,
        out_shape=jax.ShapeDtypeStruct((N, D), F32),
        compiler_params=pltpu.CompilerParams(
            dimension_semantics=("parallel",), vmem_limit_bytes=_vmem_limit(resident)),
    )(h, w_gate, w_up, w_down, g, b)


def kernel(x, mem, w_in, w_pool, pool_scale, w_out, ln1_g, ln1_b, w_xq, w_xkv, w_xo,
           ln2_g, ln2_b, w_gate, w_up, w_down, ln3_g, ln3_b):
    B, S, D = x.shape
    assert w_in.shape[0] == DEPTH and S % (4 * MOBA_BLOCK) == 0
    assert (w_in.shape[2] - POOL_WIDTH) % (3 * V7X_LANES) == 0
    h = x
    for l in range(DEPTH):
        wb = lambda w: w[l].astype(BF16)
        row = lambda p: p[l].reshape(1, -1)
        pool, q, k, v = _proj_pool(h, wb(w_in), wb(w_pool), row(pool_scale), ts=1024,
                                   chunk=512)
        attn = _moba(q, k, v)
        qk, vo = _mem_fold(mem, wb(w_xkv), wb(w_xq), wb(w_xo))
        h = _mix_xattn(h, pool, attn, wb(w_out), row(ln1_g), row(ln1_b), qk, vo,
                       row(ln2_g), row(ln2_b), ts=1024)
        h = _ffn(h.reshape(B * S, D), wb(w_gate), wb(w_up), wb(w_down),
                 row(ln3_g), row(ln3_b), ts=1024, ff_chunk=256).reshape(B, S, D)
    return h
```

```python
import functools
import math

import jax
import jax.numpy as jnp
from jax import lax
from jax.experimental import pallas as pl
from jax.experimental.pallas import tpu as pltpu

POOL_WINDOWS = (2, 4, 8, 16)
POOL_GROUP = 128
POOL_WIDTH = POOL_GROUP * len(POOL_WINDOWS)
POOL_HALO = max(POOL_WINDOWS)
ATTN_HEAD_DIM = 64
MOBA_BLOCK = 256
MOBA_TOPK = 3
XATTN_HEADS = 4
DEPTH = 1
DEEPNORM_ALPHA = (2.0 * DEPTH) ** 0.25
LN_EPS = 1e-5
LOG2E = math.log2(math.e)

V7X_LANES = 128
V7X_SUBLANES = 8
V7X_VMEM_BYTES = 64 * 1024 * 1024
V7X_BF16_ROWS = 16
HEADS_PER_SLAB = V7X_LANES // ATTN_HEAD_DIM

F32 = jnp.float32
BF16 = jnp.bfloat16
NEG_INF = float("-inf")


def _vmem_limit(resident_bytes):
    return int(min(2 * resident_bytes, V7X_VMEM_BYTES * 7 // 8))


def _const_spec(shape):
    return pl.BlockSpec(shape, lambda *_: (0,) * len(shape), pipeline_mode=pl.Buffered(1))


def _dot(a, b):
    return jnp.dot(a, b, preferred_element_type=F32)


def _layer_norm(y, g, b):
    mu = jnp.mean(y, axis=-1, keepdims=True)
    d = y - mu
    var = jnp.mean(d * d, axis=-1, keepdims=True)
    return d * lax.rsqrt(var + LN_EPS) * g + b


def _proj_pool_kernel(x_ref, w_in_ref, w_pool_ref, scale_ref,
                      pool_ref, q_ref, k_ref, v_ref, ubuf_ref, *, q_scale):
    s_idx = pl.program_id(1)
    ts = x_ref.shape[1]
    aw = q_ref.shape[2]
    chunk = ubuf_ref.shape[0] - POOL_HALO

    @pl.when(s_idx == 0)
    def _():
        ubuf_ref[0:POOL_HALO, :] = jnp.zeros((POOL_HALO, POOL_WIDTH), F32)

    for c in range(ts // chunk):
        rows = slice(c * chunk, (c + 1) * chunk)
        xb = x_ref[0, rows, :].astype(BF16)
        u = _dot(xb, w_in_ref[:, 0:POOL_WIDTH])
        ubuf_ref[POOL_HALO:POOL_HALO + chunk, :] = u
        head_pos = (s_idx * ts + c * chunk
                    + lax.broadcasted_iota(jnp.int32, (POOL_HALO, POOL_GROUP), 0))
        for g, w in enumerate(POOL_WINDOWS):
            cols = slice(g * POOL_GROUP, (g + 1) * POOL_GROUP)
            win = ubuf_ref[:, cols]
            span = 1
            while span < w:
                win = win + pltpu.roll(win, span, axis=0)
                span *= 2
            win = win[POOL_HALO:, :]
            head_inv = 1.0 / jnp.minimum(head_pos + 1, w).astype(F32)
            mean = jnp.concatenate(
                [win[0:POOL_HALO, :] * head_inv, win[POOL_HALO:, :] * (1.0 / w)], axis=0)
            mixed = _dot((mean - u[:, cols]).astype(BF16), w_pool_ref[g]) * scale_ref[:, cols]
            pool_ref[0, rows, cols] = mixed.astype(BF16)
        ubuf_ref[0:POOL_HALO, :] = u[chunk - POOL_HALO:, :]
        q_ref[0, rows, :] = (_dot(xb, w_in_ref[:, POOL_WIDTH:POOL_WIDTH + aw]) * q_scale).astype(BF16)
        k_ref[0, rows, :] = _dot(xb, w_in_ref[:, POOL_WIDTH + aw:POOL_WIDTH + 2 * aw]).astype(BF16)
        v_ref[0, rows, :] = _dot(xb, w_in_ref[:, POOL_WIDTH + 2 * aw:POOL_WIDTH + 3 * aw]).astype(BF16)


def _proj_pool(x, w_in, w_pool, pool_scale, *, ts, chunk):
    B, S, D = x.shape
    aw = (w_in.shape[1] - POOL_WIDTH) // 3
    q_scale = ATTN_HEAD_DIM ** -0.5 * LOG2E
    slab = lambda width: pl.BlockSpec((1, ts, width), lambda b, s: (b, s, 0))
    assert ts % chunk == 0 and chunk > POOL_HALO
    resident = (2 * ts * D * 4 + D * w_in.shape[1] * 2 + 2 * ts * (POOL_WIDTH + 3 * aw) * 2
                + (chunk + POOL_HALO) * POOL_WIDTH * 4 + 4 * chunk * w_in.shape[1] * 4)
    return pl.pallas_call(
        functools.partial(_proj_pool_kernel, q_scale=q_scale),
        grid=(B, S // ts),
        in_specs=[slab(D), _const_spec(w_in.shape), _const_spec(w_pool.shape),
                  _const_spec(pool_scale.shape)],
        out_specs=[slab(POOL_WIDTH), slab(aw), slab(aw), slab(aw)],
        out_shape=[jax.ShapeDtypeStruct((B, S, POOL_WIDTH), BF16)]
        + [jax.ShapeDtypeStruct((B, S, aw), BF16)] * 3,
        scratch_shapes=[pltpu.VMEM((chunk + POOL_HALO, POOL_WIDTH), F32)],
        compiler_params=pltpu.CompilerParams(
            dimension_semantics=("parallel", "arbitrary"),
            vmem_limit_bytes=_vmem_limit(resident)),
    )(x, w_in, w_pool, pool_scale)


def _moba_kernel(q_ref, k_ref, v_ref, o_ref, qt_ref, vt_ref, acc_ref,
                 s0_ref, kmean_ref, s1_ref, bias0_ref, bias1_ref,
                 meff0_ref, meff1_ref, tmax0_ref, tmax1_ref):
    S = q_ref.shape[1]
    nb = S // MOBA_BLOCK
    blk = MOBA_BLOCK
    hd = ATTN_HEAD_DIM
    heads = range(HEADS_PER_SLAB)
    n_pairs = nb // 2
    s_refs = (s0_ref, s1_ref)
    bias_refs, meff_refs = (bias0_ref, bias1_ref), (meff0_ref, meff1_ref)
    tmax_refs = (tmax0_ref, tmax1_ref)
    trows = tmax0_ref.shape[1]

    chan_head = lax.broadcasted_iota(jnp.int32, (V7X_LANES, blk), 0) // hd
    vrows = vt_ref.shape[1]
    ones_row = jnp.where(lax.broadcasted_iota(jnp.int32, (vrows - hd, blk), 0) == 0,
                         1.0, 0.0).astype(BF16)
    for c in range(nb):
        cols = slice(c * blk, (c + 1) * blk)
        qtf = q_ref[0, cols, :].astype(F32).T
        for h in heads:
            qt_ref[h, :, cols] = jnp.where(chan_head == h, qtf, 0.0).astype(BF16)
        vtb = v_ref[0, cols, :].astype(F32).T.astype(BF16)
        for h in heads:
            vt_ref[h, 0:hd, cols] = vtb[h * hd:(h + 1) * hd, :]
            vt_ref[h, hd:vrows, cols] = ones_row
        kmean_ref[c:c + 1, :] = jnp.mean(k_ref[0, cols, :].astype(F32), axis=0, keepdims=True)

    blk_idx = lax.broadcasted_iota(jnp.int32, (nb, blk), 0)
    tile_idx = lax.broadcasted_iota(jnp.int32, (trows, blk), 0)
    for tmax_ref in tmax_refs:
        tmax_ref[...] = jnp.full(tmax_ref.shape, NEG_INF, F32)
    causal = (lax.broadcasted_iota(jnp.int32, (blk, blk), 0)
              <= lax.broadcasted_iota(jnp.int32, (blk, blk), 1))

    def q_offsets(i):
        return [pl.multiple_of(qb * blk, blk) for qb in (i, nb - 1 - i)]

    def past_tile(i, t):
        first = t <= i
        sel = jnp.where(first, 0, 1)
        j = jnp.where(first, t - 1, t - i - 1)
        q_off = q_offsets(i)
        q0 = pl.multiple_of(jnp.where(first, q_off[0], q_off[1]), blk)
        return sel, q0, j, pl.multiple_of(j * blk, blk)

    def gates(i, bias_ref):
        never = jnp.full((trows - nb, blk), NEG_INF, F32)
        for h in heads:
            by_block = []
            for sel, qb in ((0, i), (1, nb - 1 - i)):
                q0 = pl.multiple_of(qb * blk, blk)
                qhf = qt_ref[h, :, pl.ds(q0, blk)].astype(F32)
                gate = jnp.dot(kmean_ref[...], qhf, preferred_element_type=F32,
                               precision=lax.Precision.HIGHEST)
                past = blk_idx < qb
                gate = jnp.where(past, gate, NEG_INF)
                bias = jnp.full((nb, blk), NEG_INF, F32)
                for _ in range(MOBA_TOPK):
                    top = jnp.max(gate, axis=0, keepdims=True)
                    lowest = jnp.min(jnp.where(gate == top, blk_idx, nb), axis=0, keepdims=True)
                    taken = blk_idx == lowest
                    bias = jnp.where(taken, 0.0, bias)
                    gate = jnp.where(taken, NEG_INF, gate)
                bias = jnp.where(past, bias, NEG_INF)
                by_block.append(jnp.concatenate([bias, never], axis=0))
            first = pltpu.roll(by_block[0], 1, axis=0)
            second = pltpu.roll(by_block[1], i + 1, axis=0)
            own = (tile_idx == 0) | (tile_idx == nb)
            bias_ref[h] = jnp.where(own, 0.0, jnp.where(tile_idx <= i, first, second))

    def score_tile(i, t, h, s_ref, tmax_ref):
        q_off = q_offsets(i)
        if t in (0, nb):
            q0 = k0 = q_off[0 if t == 0 else 1]
        else:
            _, q0, _, k0 = past_tile(i, t)
        s = _dot(k_ref[0, pl.ds(k0, blk), :], qt_ref[h, :, pl.ds(q0, blk)])
        if t in (0, nb):
            s = jnp.where(causal, s, NEG_INF)
        s_ref[h, t] = s
        tmax_ref[h, t:t + 1, :] = jnp.max(s, axis=0, keepdims=True)

    def chosen_max(i, tmax_ref, bias_ref, meff_ref):
        in_first = tile_idx <= i
        in_second = jnp.logical_and(tile_idx > i, tile_idx <= nb)
        for h in heads:
            bias = bias_ref[h]
            tm = tmax_ref[h] + bias
            m_first = jnp.max(jnp.where(in_first, tm, NEG_INF), axis=0, keepdims=True)
            m_second = jnp.max(jnp.where(in_second, tm, NEG_INF), axis=0, keepdims=True)
            meff_ref[h] = jnp.where(in_first, m_first, m_second) - bias

    def value_tile(i, t, h, s_ref, meff_ref):
        if t in (0, nb):
            sel = 0 if t == 0 else 1
            k0 = q_offsets(i)[sel]
        else:
            sel, _, _, k0 = past_tile(i, t)
        p = jnp.exp2(s_ref[h, t] - meff_ref[h, t:t + 1, :]).astype(BF16)
        acc_ref[sel, h] += _dot(vt_ref[h, :, pl.ds(k0, blk)], p)

    def store(i):
        q_off = q_offsets(i)
        for sel in range(2):
            o_t = jnp.concatenate(
                [acc_ref[sel, h, 0:hd, :] * (1.0 / acc_ref[sel, h, hd:hd + 1, :]) for h in heads],
                axis=0)
            o_ref[0, pl.ds(q_off[sel], blk), :] = o_t.T.astype(BF16)

    tile_order = [0, nb] + list(range(1, nb))

    def pipeline_step(i, par, with_scores=True):
        cur, nxt = par, 1 - par
        chosen_max(i, tmax_refs[cur], bias_refs[cur], meff_refs[cur])
        acc_ref[...] = jnp.zeros(acc_ref.shape, F32)
        for n, t in enumerate(tile_order):
            for h in heads:
                value_tile(i, t, h, s_refs[cur], meff_refs[cur])
                if with_scores:
                    score_tile(i + 1, t, h, s_refs[nxt], tmax_refs[nxt])
            if with_scores and n == len(tile_order) // 2:
                gates(i + 1, bias_refs[nxt])
        store(i)

    gates(0, bias_refs[0])
    for t in tile_order:
        for h in heads:
            score_tile(0, t, h, s_refs[0], tmax_refs[0])

    def two_pairs(d, carry):
        pipeline_step(2 * d, 0)
        pipeline_step(2 * d + 1, 1)
        return carry

    lax.fori_loop(0, n_pairs // 2 - 1, two_pairs, 0)
    pipeline_step(n_pairs - 2, 0)
    pipeline_step(n_pairs - 1, 1, with_scores=False)


def _moba(q, k, v):
    B, S, aw = q.shape
    nb = S // MOBA_BLOCK
    blk, hps, hd = MOBA_BLOCK, HEADS_PER_SLAB, ATTN_HEAD_DIM
    assert nb % 4 == 0
    trows = -(-(nb + 1) // V7X_SUBLANES) * V7X_SUBLANES
    slab = pl.BlockSpec((1, S, V7X_LANES), lambda b, g: (b, 0, g))
    per_parity = [
        pltpu.VMEM((hps, nb + 1, blk, blk), F32),
        pltpu.VMEM((hps, trows, blk), F32),
        pltpu.VMEM((hps, trows, blk), F32),
        pltpu.VMEM((hps, trows, blk), F32),
    ]
    scratch = [
        pltpu.VMEM((hps, V7X_LANES, S), BF16),
        pltpu.VMEM((hps, hd + V7X_BF16_ROWS, S), BF16),
        pltpu.VMEM((2, hps, hd + V7X_BF16_ROWS, blk), F32),
        per_parity[0],
        pltpu.VMEM((nb, V7X_LANES), F32),
        per_parity[0],
    ] + [spec for spec in per_parity[1:] for _ in range(2)]
    resident = (8 * S * V7X_LANES * 2 + (hps + 1) * S * V7X_LANES * 2
                + 2 * hps * (nb + 1) * blk * blk * 4 + 8 * blk * blk * 4)
    return pl.pallas_call(
        _moba_kernel,
        grid=(B, aw // V7X_LANES),
        in_specs=[slab, slab, slab],
        out_specs=slab,
        out_shape=jax.ShapeDtypeStruct((B, S, aw), BF16),
        scratch_shapes=scratch,
        compiler_params=pltpu.CompilerParams(
            dimension_semantics=("parallel", "parallel"),
            vmem_limit_bytes=_vmem_limit(resident)),
    )(q, k, v)


def _mem_fold_kernel(mem_ref, w_kv_ref, w_q_ref, w_o_ref, qk_ref, vo_ref, *, q_scale):
    M, D = mem_ref.shape[1], mem_ref.shape[2]
    xhd = D // XATTN_HEADS
    mb = mem_ref[0].astype(BF16)
    k = _dot(mb, w_kv_ref[:, 0:D]).astype(BF16)
    v = _dot(mb, w_kv_ref[:, D:2 * D]).astype(BF16)
    for h in range(XATTN_HEADS):
        cols = slice(h * xhd, (h + 1) * xhd)
        qk = lax.dot_general(w_q_ref[:, cols], k[:, cols], (((1,), (1,)), ((), ())),
                             preferred_element_type=F32)
        qk_ref[0, :, h * M:(h + 1) * M] = (qk * q_scale).astype(BF16)
        vo_ref[0, h * M:(h + 1) * M, :] = _dot(v[:, cols], w_o_ref[cols, :]).astype(BF16)


def _mem_fold(mem, w_xkv, w_xq, w_xo):
    B, M, D = mem.shape
    q_scale = (D // XATTN_HEADS) ** -0.5 * LOG2E
    hm = XATTN_HEADS * M
    resident = 2 * M * D * 4 + 4 * D * D * 2 + 4 * D * hm * 2 + 4 * M * D * 4
    return pl.pallas_call(
        functools.partial(_mem_fold_kernel, q_scale=q_scale),
        grid=(B,),
        in_specs=[pl.BlockSpec((1, M, D), lambda b: (b, 0, 0)), _const_spec(w_xkv.shape),
                  _const_spec(w_xq.shape), _const_spec(w_xo.shape)],
        out_specs=[pl.BlockSpec((1, D, hm), lambda b: (b, 0, 0)),
                   pl.BlockSpec((1, hm, D), lambda b: (b, 0, 0))],
        out_shape=[jax.ShapeDtypeStruct((B, D, hm), BF16), jax.ShapeDtypeStruct((B, hm, D), BF16)],
        compiler_params=pltpu.CompilerParams(
            dimension_semantics=("parallel",), vmem_limit_bytes=_vmem_limit(resident)),
    )(mem, w_xkv, w_xq, w_xo)


def _mix_xattn_kernel(x_ref, pool_ref, attn_ref, w_out_ref, g1_ref, b1_ref,
                      qk_ref, vo_ref, g2_ref, b2_ref, h_ref):
    D = x_ref.shape[2]
    pw = pool_ref.shape[2]
    M = qk_ref.shape[2] // XATTN_HEADS
    mix = _dot(pool_ref[0], w_out_ref[0:pw, :]) + _dot(attn_ref[0], w_out_ref[pw:D, :])
    h1 = _layer_norm(DEEPNORM_ALPHA * x_ref[0] + mix, g1_ref[...], b1_ref[...])

    s = _dot(h1.astype(BF16), qk_ref[0])
    probs = []
    for h in range(XATTN_HEADS):
        sh = s[:, h * M:(h + 1) * M]
        p = jnp.exp2(sh - jnp.max(sh, axis=-1, keepdims=True))
        probs.append((p * (1.0 / jnp.sum(p, axis=-1, keepdims=True))).astype(BF16))
    xo = _dot(jnp.concatenate(probs, axis=-1), vo_ref[0])
    h_ref[0] = _layer_norm(DEEPNORM_ALPHA * h1 + xo, g2_ref[...], b2_ref[...])


def _mix_xattn(x, pool, attn, w_out, g1, b1, qk, vo, g2, b2, *, ts):
    B, S, D = x.shape
    hm = qk.shape[2]
    tile = lambda width: pl.BlockSpec((1, ts, width), lambda b, s: (b, s, 0))
    vec = _const_spec((1, D))
    resident = (4 * ts * D * 4 + 4 * ts * pool.shape[2] * 2 + D * D * 2 + 4 * D * hm * 2
                + 6 * ts * D * 4)
    return pl.pallas_call(
        _mix_xattn_kernel,
        grid=(B, S // ts),
        in_specs=[tile(D), tile(pool.shape[2]), tile(attn.shape[2]), _const_spec(w_out.shape),
                  vec, vec, pl.BlockSpec((1, D, hm), lambda b, s: (b, 0, 0)),
                  pl.BlockSpec((1, hm, D), lambda b, s: (b, 0, 0)), vec, vec],
        out_specs=tile(D),
        out_shape=jax.ShapeDtypeStruct((B, S, D), F32),
        compiler_params=pltpu.CompilerParams(
            dimension_semantics=("parallel", "parallel"),
            vmem_limit_bytes=_vmem_limit(resident)),
    )(x, pool, attn, w_out, g1, b1, qk, vo, g2, b2)


def _ffn_kernel(h_ref, w_gate_ref, w_up_ref, w_down_ref, g_ref, b_ref, o_ref, *, ff_chunk):
    d_ff = w_gate_ref.shape[1]
    half = h_ref.shape[0] // 2
    for r in range(2):
        rows = slice(r * half, (r + 1) * half)
        h = h_ref[rows, :]
        hb = h.astype(BF16)
        y = jnp.zeros(h.shape, F32)
        for c in range(d_ff // ff_chunk):
            cols = slice(c * ff_chunk, (c + 1) * ff_chunk)
            gate = _dot(hb, w_gate_ref[:, cols])
            up = _dot(hb, w_up_ref[:, cols])
            act = gate * (1.0 / (1.0 + jnp.exp(-gate))) * up
            y = y + _dot(act.astype(BF16), w_down_ref[cols, :])
        o_ref[rows, :] = _layer_norm(DEEPNORM_ALPHA * h + y, g_ref[...], b_ref[...])


def _ffn(h, w_gate, w_up, w_down, g, b, *, ts, ff_chunk):
    N, D = h.shape
    d_ff = w_gate.shape[1]
    tile = pl.BlockSpec((ts, D), lambda t: (t, 0))
    vec = _const_spec((1, D))
    resident = 4 * ts * D * 4 + 3 * D * d_ff * 2 + 3 * ts * ff_chunk * 4 + 2 * ts * D * 4
    return pl.pallas_call(
        functools.partial(_ffn_kernel, ff_chunk=ff_chunk),
        grid=(N // ts,),
        in_specs=[tile, _const_spec(w_gate.shape), _const_spec(w_up.shape),
                  _const_spec(w_down.shape), vec, vec],
        out_specs=tile,
        out_shape=jax.ShapeDtypeStruct((N, D), F32),
        compiler_params=pltpu.CompilerParams(
            dimension_semantics=("parallel",), vmem_limit_bytes=_vmem_limit(resident)),
    )(h, w_gate, w_up, w_down, g, b)


def kernel(x, mem, w_in, w_pool, pool_scale, w_out, ln1_g, ln1_b, w_xq, w_xkv, w_xo,
           ln2_g, ln2_b, w_gate, w_up, w_down, ln3_g, ln3_b):
    B, S, D = x.shape
    assert w_in.shape[0] == DEPTH and S % (4 * MOBA_BLOCK) == 0
    assert (w_in.shape[2] - POOL_WIDTH) % (3 * V7X_LANES) == 0
    h = x
    for l in range(DEPTH):
        wb = lambda w: w[l].astype(BF16)
        row = lambda p: p[l].reshape(1, -1)
        pool, q, k, v = _proj_pool(h, wb(w_in), wb(w_pool), row(pool_scale), ts=1024,
                                   chunk=512)
        attn = _moba(q, k, v)
        qk, vo = _mem_fold(mem, wb(w_xkv), wb(w_xq), wb(w_xo))
        h = _mix_xattn(h, pool, attn, wb(w_out), row(ln1_g), row(ln1_b), qk, vo,
                       row(ln2_g), row(ln2_b), ts=1024)
        h = _ffn(h.reshape(B * S, D), wb(w_gate), wb(w_up), wb(w_down),
                 row(ln3_g), row(ln3_b), ts=1024, ff_chunk=256).reshape(B, S, D)
    return h
```
